```python
import math
import jax, jax.numpy as jnp
from jax import lax
import numpy as np

D_MODEL = 1024
BATCH = 8
SEQ = 4096
DEPTH = 1

N_Q_HEADS = 8
HEAD_DIM = 64
N_KV_GROUPS = 2
ATTN_WIDTH = N_Q_HEADS * HEAD_DIM
CMP_BLOCK = 32
CMP_STRIDE = 16
CMP_HIDDEN = 256
SEL_BLOCK = 64
SEL_TOPN = 16
WINDOW = 512
Q_BLOCK = 128
LRU_WIDTH = D_MODEL - ATTN_WIDTH
LRU_BLOCKS = 8
LRU_BLOCK_DIM = LRU_WIDTH // LRU_BLOCKS
CONV_WIDTH = 4
LRU_C = 8.0
IN_WIDTH = ATTN_WIDTH + 6 * N_KV_GROUPS * HEAD_DIM + 3 * N_Q_HEADS + 2 * LRU_WIDTH
D_FF = 2816
NORM_EPS = 1e-6

kernel_name = 'hymba_nsa_rglru_macaron_sandwich'


def rms_norm(x, g):
    xf = x.astype(jnp.float32)
    y = xf * lax.rsqrt(jnp.mean(xf * xf, axis=-1, keepdims=True) + NORM_EPS)
    return (y * g.astype(jnp.float32)).astype(x.dtype)


def swiglu(x, w_gate, w_up, w_down):
    return (jax.nn.silu(x @ w_gate) * (x @ w_up)) @ w_down


def alibi_slopes(n):
    return jnp.asarray(np.power(2.0, -8.0 * np.arange(1, n + 1) / n), dtype=jnp.float32)


def masked_softmax(s, mask):
    s = jnp.where(mask, s, -jnp.inf)
    m = jnp.max(s, axis=-1, keepdims=True)
    m = jnp.where(jnp.isfinite(m), m, 0.0)
    p = jnp.where(mask, jnp.exp(s - m), 0.0)
    return p / jnp.maximum(jnp.sum(p, axis=-1, keepdims=True), 1e-30)


def compress_blocks(t, pe, w1, w2):
    B, T, G, dh = t.shape
    nc = (T - CMP_BLOCK) // CMP_STRIDE + 1
    idx = jnp.arange(nc)[:, None] * CMP_STRIDE + jnp.arange(CMP_BLOCK)[None, :]
    blk = t[:, idx] + pe[None, None, :, None, :]
    blk = blk.transpose(0, 3, 1, 2, 4).reshape(B, G, nc, CMP_BLOCK * dh)
    return jax.nn.gelu(blk @ w1) @ w2


def cmp_to_sel_matrix(n_cmp, n_sel):
    cs = (jnp.arange(n_cmp) * CMP_STRIDE)[:, None]
    js = (jnp.arange(n_sel) * SEL_BLOCK)[None, :]
    ov = jnp.minimum(cs + CMP_BLOCK, js + SEL_BLOCK) - jnp.maximum(cs, js)
    return jnp.maximum(ov, 0).astype(jnp.float32) / CMP_BLOCK


def nsa_attention(q, k_cmp, v_cmp, k_sel, v_sel, k_win, v_win, gates):
    B, T, G, R, dh = q.shape
    NC = k_cmp.shape[2]
    NS = k_sel.shape[2]
    NQB = T // Q_BLOCK
    n_top = min(SEL_TOPN, NS)
    scale = dh ** -0.5
    slopes = alibi_slopes(G * R).reshape(G, R)[None, :, :, None, None]
    cmp_end = jnp.arange(NC) * CMP_STRIDE + (CMP_BLOCK - 1)
    m_cs = cmp_to_sel_matrix(NC, NS)
    pad = ((0, 0), (0, 0), (WINDOW, 0), (0, 0))
    k_win_p = jnp.pad(k_win, pad)
    v_win_p = jnp.pad(v_win, pad)
    b_idx = jnp.arange(B)[:, None, None, None]
    g_idx = jnp.arange(G)[None, :, None, None]
    blk_ids = jnp.arange(NS)
    qb = q.reshape(B, NQB, Q_BLOCK, G, R, dh).transpose(1, 0, 3, 4, 2, 5)
    gb = gates.reshape(B, NQB, Q_BLOCK, G, R, 3).transpose(1, 0, 3, 4, 2, 5)
    q0s = jnp.arange(NQB, dtype=jnp.int32) * Q_BLOCK

    def block(args):
        qi, gq, q0 = args
        t = q0 + jnp.arange(Q_BLOCK)
        s = jnp.einsum('bgrqd,bgcd->bgrqc', qi, k_cmp).astype(jnp.float32) * scale
        dist = t[:, None] - cmp_end[None, :]
        s = s - slopes * dist.astype(jnp.float32)
        p_cmp = masked_softmax(s, dist >= 0)
        o_cmp = jnp.einsum('bgrqc,bgcd->bgrqd', p_cmp.astype(v_cmp.dtype), v_cmp)
        imp = jnp.einsum('bgrqc,cs->bgqs', p_cmp, m_cs)
        cur = (t // SEL_BLOCK)[:, None]
        valid = blk_ids[None, :] * SEL_BLOCK <= t[:, None]
        forced = (blk_ids[None, :] == 0) | (blk_ids[None, :] == cur) | (blk_ids[None, :] == cur - 1)
        imp = jnp.where(forced, jnp.inf, jnp.where(valid, imp, -jnp.inf))
        _, sel = lax.top_k(imp, n_top)
        kg = k_sel[b_idx, g_idx, sel]
        vg = v_sel[b_idx, g_idx, sel]
        pos = sel[..., None] * SEL_BLOCK + jnp.arange(SEL_BLOCK)
        dist = (t[None, None, :, None, None] - pos)[:, :, None]
        s = jnp.einsum('bgrqd,bgqnkd->bgrqnk', qi, kg).astype(jnp.float32) * scale
        s = s - slopes[..., None] * dist.astype(jnp.float32)
        flat = n_top * SEL_BLOCK
        p = masked_softmax(s.reshape(B, G, R, Q_BLOCK, flat),
                           (dist >= 0).reshape(B, G, 1, Q_BLOCK, flat)).reshape(s.shape)
        o_sel = jnp.einsum('bgrqnk,bgqnkd->bgrqd', p.astype(vg.dtype), vg)
        kw = lax.dynamic_slice_in_dim(k_win_p, q0, Q_BLOCK + WINDOW, axis=2)
        vw = lax.dynamic_slice_in_dim(v_win_p, q0, Q_BLOCK + WINDOW, axis=2)
        spos = q0 - WINDOW + jnp.arange(Q_BLOCK + WINDOW)
        dist = t[:, None] - spos[None, :]
        mask = (dist >= 0) & (dist < WINDOW) & (spos >= 0)[None, :]
        s = jnp.einsum('bgrqd,bgkd->bgrqk', qi, kw).astype(jnp.float32) * scale
        s = s - slopes * dist.astype(jnp.float32)
        p = masked_softmax(s, mask)
        o_win = jnp.einsum('bgrqk,bgkd->bgrqd', p.astype(vw.dtype), vw)
        g = gq.astype(jnp.float32)
        out = (g[..., 0:1] * o_cmp.astype(jnp.float32) + g[..., 1:2] * o_sel.astype(jnp.float32)
               + g[..., 2:3] * o_win.astype(jnp.float32))
        return out.astype(q.dtype)

    ob = lax.map(block, (qb, gb, q0s))
    return ob.transpose(1, 0, 4, 2, 3, 5).reshape(B, T, G, R, dh)


def _lru_combine(c1, c2):
    a1, b1 = c1
    a2, b2 = c2
    return a1 * a2, a2 * b1 + b2


def rg_lru(xr, xg, conv_w, conv_b, w_a, b_a, w_x, b_x, lam):
    B, T, C = xr.shape
    xc = lax.conv_general_dilated(xr, conv_w[:, None, :].astype(xr.dtype), window_strides=(1,),
                                  padding=[(CONV_WIDTH - 1, 0)],
                                  dimension_numbers=('NWC', 'WIO', 'NWC'),
                                  feature_group_count=C) + conv_b
    xb = xc.reshape(B, T, LRU_BLOCKS, LRU_BLOCK_DIM)
    r = jax.nn.sigmoid((jnp.einsum('btnd,nde->btne', xb, w_a).reshape(B, T, C) + b_a).astype(jnp.float32))
    i = jax.nn.sigmoid((jnp.einsum('btnd,nde->btne', xb, w_x).reshape(B, T, C) + b_x).astype(jnp.float32))
    log_a = -LRU_C * r * jax.nn.softplus(-lam.astype(jnp.float32))
    a = jnp.exp(log_a)
    mult = jnp.sqrt(jnp.maximum(-jnp.expm1(2.0 * log_a), 0.0))
    b = mult * i * xc.astype(jnp.float32)
    _, hs = lax.associative_scan(_lru_combine, (a, b), axis=1)
    return (hs * jax.nn.gelu(xg.astype(jnp.float32))).astype(xr.dtype)


def hybrid_mixer(h, w_in, cmp_k_pe, cmp_k_w1, cmp_k_w2, cmp_v_pe, cmp_v_w1, cmp_v_w2,
                 conv_w, conv_b, lru_w_a, lru_b_a, lru_w_x, lru_b_x, lru_lambda,
                 attn_out_g, lru_out_g, w_out):
    B, T, _ = h.shape
    G, dh = N_KV_GROUPS, HEAD_DIM
    R = N_Q_HEADS // G
    kv = G * dh
    sizes = [ATTN_WIDTH, kv, kv, kv, kv, kv, kv, 3 * N_Q_HEADS, LRU_WIDTH, LRU_WIDTH]
    cuts = np.cumsum(sizes)[:-1].tolist()
    q, kc, vc, ks, vs, kw, vw, gl, xr, xg = jnp.split(h @ w_in, cuts, axis=-1)
    q = q.reshape(B, T, G, R, dh)
    k_cmp = compress_blocks(kc.reshape(B, T, G, dh), cmp_k_pe, cmp_k_w1, cmp_k_w2)
    v_cmp = compress_blocks(vc.reshape(B, T, G, dh), cmp_v_pe, cmp_v_w1, cmp_v_w2)
    NS = T // SEL_BLOCK
    k_sel = ks.reshape(B, NS, SEL_BLOCK, G, dh).transpose(0, 3, 1, 2, 4)
    v_sel = vs.reshape(B, NS, SEL_BLOCK, G, dh).transpose(0, 3, 1, 2, 4)
    k_win = kw.reshape(B, T, G, dh).transpose(0, 2, 1, 3)
    v_win = vw.reshape(B, T, G, dh).transpose(0, 2, 1, 3)
    gates = jax.nn.sigmoid(gl.astype(jnp.float32)).reshape(B, T, G, R, 3)
    attn = nsa_attention(q, k_cmp, v_cmp, k_sel, v_sel, k_win, v_win, gates).reshape(B, T, ATTN_WIDTH)
    lru = rg_lru(xr, xg, conv_w, conv_b, lru_w_a, lru_b_a, lru_w_x, lru_b_x, lru_lambda)
    y = jnp.concatenate([rms_norm(attn, attn_out_g), rms_norm(lru, lru_out_g)], axis=-1)
    return y @ w_out


def setup_inputs(seed: int = 0) -> dict:
    key = jax.random.key(seed)
    keys = iter(jax.random.split(key, 40))

    def nrm(shape, scale):
        return jax.random.normal(next(keys), shape, jnp.float32) * scale

    def gain(n):
        return 1.0 + nrm((DEPTH, n), 0.02)

    D, L = D_MODEL, DEPTH
    u = jax.random.uniform(next(keys), (L, LRU_WIDTH), jnp.float32, minval=0.9, maxval=0.999)
    sig = u ** (1.0 / LRU_C)
    lru_lambda = jnp.log(sig) - jnp.log1p(-sig)
    return {
        'x': nrm((BATCH, SEQ, D), 1.0),
        'ffn1_pre_g': gain(D), 'ffn1_post_g': gain(D),
        'ffn1_w_gate': nrm((L, D, D_FF), D ** -0.5), 'ffn1_w_up': nrm((L, D, D_FF), D ** -0.5),
        'ffn1_w_down': nrm((L, D_FF, D), D_FF ** -0.5),
        'mix_pre_g': gain(D), 'mix_post_g': gain(D),
        'w_in': nrm((L, D, IN_WIDTH), D ** -0.5),
        'cmp_k_pe': nrm((L, CMP_BLOCK, HEAD_DIM), 0.1),
        'cmp_k_w1': nrm((L, CMP_BLOCK * HEAD_DIM, CMP_HIDDEN), (CMP_BLOCK * HEAD_DIM) ** -0.5),
        'cmp_k_w2': nrm((L, CMP_HIDDEN, HEAD_DIM), CMP_HIDDEN ** -0.5),
        'cmp_v_pe': nrm((L, CMP_BLOCK, HEAD_DIM), 0.1),
        'cmp_v_w1': nrm((L, CMP_BLOCK * HEAD_DIM, CMP_HIDDEN), (CMP_BLOCK * HEAD_DIM) ** -0.5),
        'cmp_v_w2': nrm((L, CMP_HIDDEN, HEAD_DIM), CMP_HIDDEN ** -0.5),
        'conv_w': nrm((L, CONV_WIDTH, LRU_WIDTH), CONV_WIDTH ** -0.5),
        'conv_b': nrm((L, LRU_WIDTH), 0.01),
        'lru_w_a': nrm((L, LRU_BLOCKS, LRU_BLOCK_DIM, LRU_BLOCK_DIM), LRU_BLOCK_DIM ** -0.5),
        'lru_b_a': nrm((L, LRU_WIDTH), 0.01),
        'lru_w_x': nrm((L, LRU_BLOCKS, LRU_BLOCK_DIM, LRU_BLOCK_DIM), LRU_BLOCK_DIM ** -0.5),
        'lru_b_x': nrm((L, LRU_WIDTH), 0.01),
        'lru_lambda': lru_lambda,
        'attn_out_g': gain(ATTN_WIDTH), 'lru_out_g': gain(LRU_WIDTH),
        'w_out': nrm((L, D, D), D ** -0.5),
        'ffn2_pre_g': gain(D), 'ffn2_post_g': gain(D),
        'ffn2_w_gate': nrm((L, D, D_FF), D ** -0.5), 'ffn2_w_up': nrm((L, D, D_FF), D ** -0.5),
        'ffn2_w_down': nrm((L, D_FF, D), D_FF ** -0.5),
    }


def reference(x, ffn1_pre_g, ffn1_post_g, ffn1_w_gate, ffn1_w_up, ffn1_w_down,
              mix_pre_g, mix_post_g, w_in, cmp_k_pe, cmp_k_w1, cmp_k_w2,
              cmp_v_pe, cmp_v_w1, cmp_v_w2, conv_w, conv_b, lru_w_a, lru_b_a,
              lru_w_x, lru_b_x, lru_lambda, attn_out_g, lru_out_g, w_out,
              ffn2_pre_g, ffn2_post_g, ffn2_w_gate, ffn2_w_up, ffn2_w_down):
    h = x
    for l in range(DEPTH):
        f1 = swiglu(rms_norm(h, ffn1_pre_g[l]), ffn1_w_gate[l], ffn1_w_up[l], ffn1_w_down[l])
        h = h + 0.5 * rms_norm(f1, ffn1_post_g[l])
        m = hybrid_mixer(rms_norm(h, mix_pre_g[l]), w_in[l], cmp_k_pe[l], cmp_k_w1[l], cmp_k_w2[l],
                         cmp_v_pe[l], cmp_v_w1[l], cmp_v_w2[l], conv_w[l], conv_b[l],
                         lru_w_a[l], lru_b_a[l], lru_w_x[l], lru_b_x[l], lru_lambda[l],
                         attn_out_g[l], lru_out_g[l], w_out[l])
        h = h + rms_norm(m, mix_post_g[l])
        f2 = swiglu(rms_norm(h, ffn2_pre_g[l]), ffn2_w_gate[l], ffn2_w_up[l], ffn2_w_down[l])
        h = h + 0.5 * rms_norm(f2, ffn2_post_g[l])
    return h
```

```python
import functools

import numpy as np
import jax
import jax.numpy as jnp
from jax import lax
from jax.experimental import pallas as pl
from jax.experimental.pallas import tpu as pltpu

F32 = jnp.float32
BF16 = jnp.bfloat16

D_MODEL = 1024
N_Q_HEADS = 8
HEAD_DIM = 64
N_KV_GROUPS = 2
HEADS_PER_GROUP = N_Q_HEADS // N_KV_GROUPS
ATTN_WIDTH = N_Q_HEADS * HEAD_DIM
GROUP_WIDTH = HEADS_PER_GROUP * HEAD_DIM
CMP_BLOCK = 32
CMP_STRIDE = 16
CMP_HIDDEN = 256
SEL_BLOCK = 64
SEL_TOPN = 16
WINDOW = 512
Q_BLOCK = 128
LRU_WIDTH = 512
LRU_BLOCKS = 8
CONV_WIDTH = 4
LRU_C = 8.0
D_FF = 2816
NORM_EPS = 1e-6

LANES = 128
SUBLANES = 8
VMEM_LIMIT_BYTES = 56 * 1024 * 1024

NEG_BIG = -1e30
FORCED_SCORE = 3e38

FFN_TOKENS = 512
FFN_CHUNK = 256
PROJ_TOKENS = 512
LRU_TOKENS = 512
SEL_CHUNK = 512
WIN_KEYS = WINDOW + Q_BLOCK

COL_Q = 0
COL_CMP = COL_Q + ATTN_WIDTH
COL_KV = COL_CMP + 2 * N_KV_GROUPS * HEAD_DIM
COL_GATE = COL_KV + 4 * N_KV_GROUPS * HEAD_DIM
COL_XR = COL_GATE + N_KV_GROUPS * LANES
COL_XG = COL_XR + LRU_WIDTH
PROJ_WIDTH = COL_XG + LRU_WIDTH


def _rms(x, g):
    ms = jnp.mean(x * x, axis=-1, keepdims=True)
    return x * lax.rsqrt(ms + NORM_EPS) * g


def _const_spec(shape):
    nd = len(shape)
    return pl.BlockSpec(shape, lambda *_: (0,) * nd, pipeline_mode=pl.Buffered(1))


def _params(*sem):
    return pltpu.CompilerParams(dimension_semantics=sem, vmem_limit_bytes=VMEM_LIMIT_BYTES)


def _ffn_kernel(x_ref, pre_g_ref, post_g_ref, wg_ref, wu_ref, wd_ref, o_ref, acc_ref):
    x = x_ref[...]
    xb = _rms(x, pre_g_ref[...]).astype(BF16)
    for c in range(D_FF // FFN_CHUNK):
        sl = slice(c * FFN_CHUNK, (c + 1) * FFN_CHUNK)
        gate = jnp.dot(xb, wg_ref[:, sl], preferred_element_type=F32)
        up = jnp.dot(xb, wu_ref[:, sl], preferred_element_type=F32)
        act = (jax.nn.silu(gate) * up).astype(BF16)
        contrib = jnp.dot(act, wd_ref[sl, :], preferred_element_type=F32)
        if c == 0:
            acc_ref[...] = contrib
        else:
            acc_ref[...] += contrib
    o_ref[...] = x + 0.5 * _rms(acc_ref[...], post_g_ref[...])


def _ffn(h, pre_g, post_g, w_gate, w_up, w_down):
    n = h.shape[0]
    tok = pl.BlockSpec((FFN_TOKENS, D_MODEL), lambda i: (i, 0))
    return pl.pallas_call(
        _ffn_kernel,
        grid=(n // FFN_TOKENS,),
        in_specs=[tok, _const_spec((1, D_MODEL)), _const_spec((1, D_MODEL)),
                  _const_spec((D_MODEL, D_FF)), _const_spec((D_MODEL, D_FF)),
                  _const_spec((D_FF, D_MODEL))],
        out_specs=tok,
        out_shape=jax.ShapeDtypeStruct((n, D_MODEL), F32),
        scratch_shapes=[pltpu.VMEM((FFN_TOKENS, D_MODEL), F32)],
        compiler_params=_params("parallel"),
        name="ffn",
    )(h, pre_g, post_g, w_gate, w_up, w_down)


def _proj_kernel(h_ref, g_ref, w_ref, q_ref, cmp_ref, kv_ref, gate_ref, xr_ref, xg_ref):
    hb = _rms(h_ref[0], g_ref[...]).astype(BF16)
    p = jnp.dot(hb, w_ref[...], preferred_element_type=F32)
    q_ref[0] = (p[:, COL_Q:COL_CMP] * (HEAD_DIM ** -0.5)).astype(BF16)
    cmp_ref[0] = p[:, COL_CMP:COL_KV]
    for i in range(4 * N_KV_GROUPS):
        lo = COL_KV + i * HEAD_DIM
        kv_ref[0, i] = p[:, lo:lo + HEAD_DIM].astype(BF16)
    gate_ref[0] = jax.nn.sigmoid(p[:, COL_GATE:COL_XR])
    xr_ref[0] = p[:, COL_XR:COL_XG]
    xg_ref[0] = p[:, COL_XG:PROJ_WIDTH]


def _proj(h, g, w_packed):
    b, t, _ = h.shape
    tm = PROJ_TOKENS

    def tok(width):
        return pl.BlockSpec((1, tm, width), lambda bi, ti: (bi, ti, 0))

    return pl.pallas_call(
        _proj_kernel,
        grid=(b, t // tm),
        in_specs=[tok(D_MODEL), _const_spec((1, D_MODEL)), _const_spec((D_MODEL, PROJ_WIDTH))],
        out_specs=[
            tok(ATTN_WIDTH),
            tok(2 * N_KV_GROUPS * HEAD_DIM),
            pl.BlockSpec((1, 4 * N_KV_GROUPS, tm, HEAD_DIM), lambda bi, ti: (bi, 0, ti, 0)),
            tok(N_KV_GROUPS * LANES),
            tok(LRU_WIDTH),
            tok(LRU_WIDTH),
        ],
        out_shape=[
            jax.ShapeDtypeStruct((b, t, ATTN_WIDTH), BF16),
            jax.ShapeDtypeStruct((b, t, 2 * N_KV_GROUPS * HEAD_DIM), F32),
            jax.ShapeDtypeStruct((b, 4 * N_KV_GROUPS, t, HEAD_DIM), BF16),
            jax.ShapeDtypeStruct((b, t, N_KV_GROUPS * LANES), F32),
            jax.ShapeDtypeStruct((b, t, LRU_WIDTH), F32),
            jax.ShapeDtypeStruct((b, t, LRU_WIDTH), F32),
        ],
        compiler_params=_params("parallel", "parallel"),
        name="proj",
    )(h, g, w_packed)


def _compress_kernel(xk_ref, xv_ref, kpe_ref, kw1_ref, kw2_ref, vpe_ref, vw1_ref, vw2_ref, o_ref):
    n_chunks = xk_ref.shape[1] // CMP_STRIDE
    half = CMP_BLOCK // 2
    kinds = ((xk_ref, kpe_ref, kw1_ref, kw2_ref), (xv_ref, vpe_ref, vw1_ref, vw2_ref))
    top = [jnp.zeros((n_chunks, CMP_HIDDEN), F32) for _ in range(4)]
    bot = [jnp.zeros((n_chunks, CMP_HIDDEN), F32) for _ in range(4)]
    for l in range(half):
        rows = [ref[0, pl.ds(l, n_chunks, stride=CMP_STRIDE), :] for ref in (xk_ref, xv_ref)]
        for s in range(4):
            _, pe_ref, w1_ref, _ = kinds[s // N_KV_GROUPS]
            gi = s % N_KV_GROUPS
            xs = rows[s // N_KV_GROUPS][:, gi * HEAD_DIM:(gi + 1) * HEAD_DIM]
            x_top = (xs + pe_ref[l:l + 1, :]).astype(BF16)
            x_bot = (xs + pe_ref[half + l:half + l + 1, :]).astype(BF16)
            top[s] += jnp.dot(x_top, w1_ref[l * HEAD_DIM:(l + 1) * HEAD_DIM, :],
                              preferred_element_type=F32)
            bot[s] += jnp.dot(x_bot, w1_ref[(half + l) * HEAD_DIM:(half + l + 1) * HEAD_DIM, :],
                              preferred_element_type=F32)
    row = lax.broadcasted_iota(jnp.int32, (n_chunks, HEAD_DIM), 0)
    for s in range(4):
        w2_ref = kinds[s // N_KV_GROUPS][3]
        hidden = top[s] + pltpu.roll(bot[s], n_chunks - 1, axis=0)
        out = jnp.dot(jax.nn.gelu(hidden).astype(BF16), w2_ref[...], preferred_element_type=F32)
        o_ref[0, s] = jnp.where(row < n_chunks - 1, out, 0.0).astype(BF16)


def _compress(cmp_in, k_pe, k_w1, k_w2, v_pe, v_w1, v_w2):
    b, t, _ = cmp_in.shape
    n_chunks = t // CMP_STRIDE
    kv_cols = N_KV_GROUPS * HEAD_DIM
    return pl.pallas_call(
        _compress_kernel,
        grid=(b,),
        in_specs=[pl.BlockSpec((1, t, kv_cols), lambda bi: (bi, 0, 0)),
                  pl.BlockSpec((1, t, kv_cols), lambda bi: (bi, 0, 1)),
                  _const_spec(k_pe.shape), _const_spec(k_w1.shape), _const_spec(k_w2.shape),
                  _const_spec(v_pe.shape), _const_spec(v_w1.shape), _const_spec(v_w2.shape)],
        out_specs=pl.BlockSpec((1, 2 * N_KV_GROUPS, n_chunks, HEAD_DIM), lambda bi: (bi, 0, 0, 0)),
        out_shape=jax.ShapeDtypeStruct((b, 2 * N_KV_GROUPS, n_chunks, HEAD_DIM), BF16),
        compiler_params=_params("parallel"),
        name="compress",
    )(cmp_in, cmp_in, k_pe, k_w1, k_w2, v_pe, v_w1, v_w2)


_NT = (((1,), (1,)), ((), ()))


def _attn_kernel(q_ref, kc_ref, vc_ref, ks_ref, vs_ref, kw_ref, vw_ref, gate_ref,
                 mcs_t_ref, expand_ref, o_ref, bias_ref):
    g = pl.program_id(1)
    qb = pl.program_id(2)
    q0 = qb * Q_BLOCK
    rows = HEADS_PER_GROUP * Q_BLOCK
    n_cmp = kc_ref.shape[2]
    n_sel = mcs_t_ref.shape[0]

    qblk = q_ref[0]
    q4 = jnp.concatenate([qblk[:, r * HEAD_DIM:(r + 1) * HEAD_DIM]
                          for r in range(HEADS_PER_GROUP)], axis=0)
    row = lax.broadcasted_iota(jnp.int32, (rows, 1), 0)
    t_row = q0 + (row & (Q_BLOCK - 1))
    head = g * HEADS_PER_GROUP + (row >> 7)
    slope = lax.bitcast_convert_type((126 - head) << 23, F32)

    s = lax.dot_general(q4, kc_ref[0, 0], _NT, preferred_element_type=F32)
    cmp_end = lax.broadcasted_iota(jnp.int32, (1, n_cmp), 1) * CMP_STRIDE + (CMP_BLOCK - 1)
    dist = t_row - cmp_end
    s = s - slope * dist.astype(F32)
    mask = dist >= 0
    s = jnp.where(mask, s, NEG_BIG)
    m = jnp.max(s, axis=-1, keepdims=True)
    p = jnp.where(mask, jnp.exp(s - m), 0.0)
    p = p / jnp.maximum(jnp.sum(p, axis=-1, keepdims=True), 1e-30)
    pb = p.astype(BF16)
    o_cmp = jnp.dot(pb, vc_ref[0, 0], preferred_element_type=F32)

    imp4 = lax.dot_general(mcs_t_ref[...], pb, _NT, preferred_element_type=F32)
    imp = imp4[:, 0:Q_BLOCK]
    for r in range(1, HEADS_PER_GROUP):
        imp = imp + imp4[:, r * Q_BLOCK:(r + 1) * Q_BLOCK]
    blk = lax.broadcasted_iota(jnp.int32, (n_sel, Q_BLOCK), 0)
    tq = q0 + lax.broadcasted_iota(jnp.int32, (n_sel, Q_BLOCK), 1)
    cur = tq >> 6
    forced = (blk == 0) | (blk == cur) | (blk == cur - 1)
    valid = blk * SEL_BLOCK <= tq
    score = jnp.where(forced, FORCED_SCORE, jnp.where(valid, imp, -1.0))
    rank = jnp.zeros((n_sel, Q_BLOCK), F32)
    for j in range(n_sel):
        other = score[j:j + 1, :]
        rank = rank + jnp.where(blk > j, jnp.where(other >= score, 1.0, 0.0),
                                jnp.where(other > score, 1.0, 0.0))
    chosen_t = jnp.where(rank < float(min(SEL_TOPN, n_sel)), 1.0, 0.0)
    chosen = chosen_t.T.astype(BF16)
    n_chunks_total = bias_ref.shape[0]
    for c in range(n_chunks_total):
        hit = jnp.dot(chosen, expand_ref[:, c * SEL_CHUNK:(c + 1) * SEL_CHUNK],
                      preferred_element_type=F32)
        bias_ref[c] = (hit - 1.0) * (-NEG_BIG)

    def sel_body(c, carry):
        m_i, l_i, acc = carry
        k0 = pl.multiple_of(c * SEL_CHUNK, SEL_CHUNK)
        k = ks_ref[0, 0, pl.ds(k0, SEL_CHUNK), :]
        v = vs_ref[0, 0, pl.ds(k0, SEL_CHUNK), :]
        sc = lax.dot_general(q4, k, _NT, preferred_element_type=F32)
        pos = k0 + lax.broadcasted_iota(jnp.int32, (1, SEL_CHUNK), 1)
        d = t_row - pos
        bias = bias_ref[c]
        bias4 = jnp.concatenate([bias] * HEADS_PER_GROUP, axis=0)
        sc = jnp.where(d >= 0, sc - slope * d.astype(F32) + bias4, NEG_BIG)
        m_new = jnp.maximum(m_i, jnp.max(sc, axis=-1, keepdims=True))
        alpha = jnp.exp(m_i - m_new)
        pc = jnp.exp(sc - m_new)
        l_new = alpha * l_i + jnp.sum(pc, axis=-1, keepdims=True)
        acc_new = alpha * acc + jnp.dot(pc.astype(BF16), v, preferred_element_type=F32)
        return m_new, l_new, acc_new

    n_live = (q0 + Q_BLOCK - 1) // SEL_CHUNK + 1
    init = (jnp.full((rows, 1), NEG_BIG, F32), jnp.zeros((rows, 1), F32),
            jnp.zeros((rows, HEAD_DIM), F32))
    _, l_sel, acc_sel = lax.fori_loop(0, n_live, sel_body, init)
    o_sel = acc_sel / l_sel

    w0 = pl.multiple_of(jnp.maximum(q0 - WINDOW, 0), Q_BLOCK)
    kwin = kw_ref[0, 0, pl.ds(w0, WIN_KEYS), :]
    vwin = vw_ref[0, 0, pl.ds(w0, WIN_KEYS), :]
    sw = lax.dot_general(q4, kwin, _NT, preferred_element_type=F32)
    pos = w0 + lax.broadcasted_iota(jnp.int32, (1, WIN_KEYS), 1)
    d = t_row - pos
    wmask = (d >= 0) & (d < WINDOW)
    sw = jnp.where(wmask, sw - slope * d.astype(F32), NEG_BIG)
    mw = jnp.max(sw, axis=-1, keepdims=True)
    pw = jnp.exp(sw - mw)
    lw = jnp.sum(pw, axis=-1, keepdims=True)
    o_win = jnp.dot(pw.astype(BF16), vwin, preferred_element_type=F32) / lw

    gates = gate_ref[0]
    outs = []
    for r in range(HEADS_PER_GROUP):
        rs = slice(r * Q_BLOCK, (r + 1) * Q_BLOCK)
        outs.append(gates[:, 3 * r:3 * r + 1] * o_cmp[rs]
                    + gates[:, 3 * r + 1:3 * r + 2] * o_sel[rs]
                    + gates[:, 3 * r + 2:3 * r + 3] * o_win[rs])
    o_ref[0] = jnp.concatenate(outs, axis=-1)


def _attention(q, cmp_kv, kv, gates):
    b, t, _ = q.shape
    n_cmp = cmp_kv.shape[2]
    n_sel = t // SEL_BLOCK
    n_real_cmp = (t - CMP_BLOCK) // CMP_STRIDE + 1

    cs = np.arange(n_cmp)[None, :] * CMP_STRIDE
    js = np.arange(n_sel)[:, None] * SEL_BLOCK
    ov = np.minimum(cs + CMP_BLOCK, js + SEL_BLOCK) - np.maximum(cs, js)
    mcs_t = np.maximum(ov, 0).astype(np.float32) / CMP_BLOCK
    mcs_t[:, n_real_cmp:] = 0.0
    expand = (np.arange(t)[None, :] // SEL_BLOCK == np.arange(n_sel)[:, None]).astype(np.float32)

    def kv_spec(kind, rows):
        return pl.BlockSpec((1, 1, rows, HEAD_DIM),
                            lambda bi, gi, qi, kind=kind: (bi, kind * N_KV_GROUPS + gi, 0, 0))

    return pl.pallas_call(
        _attn_kernel,
        grid=(b, N_KV_GROUPS, t // Q_BLOCK),
        in_specs=[
            pl.BlockSpec((1, Q_BLOCK, GROUP_WIDTH), lambda bi, gi, qi: (bi, qi, gi)),
            kv_spec(0, n_cmp), kv_spec(1, n_cmp),
            kv_spec(0, t), kv_spec(1, t), kv_spec(2, t), kv_spec(3, t),
            pl.BlockSpec((1, Q_BLOCK, LANES), lambda bi, gi, qi: (bi, qi, gi)),
            _const_spec(mcs_t.shape), _const_spec(expand.shape),
        ],
        out_specs=pl.BlockSpec((1, Q_BLOCK, GROUP_WIDTH), lambda bi, gi, qi: (bi, qi, gi)),
        out_shape=jax.ShapeDtypeStruct((b, t, ATTN_WIDTH), F32),
        scratch_shapes=[pltpu.VMEM((t // SEL_CHUNK, Q_BLOCK, SEL_CHUNK), F32)],
        compiler_params=_params("parallel", "parallel", "arbitrary"),
        name="nsa_attn",
    )(q, cmp_kv, cmp_kv, kv, kv, kv, kv, gates,
      jnp.asarray(mcs_t, BF16), jnp.asarray(expand, BF16))


def _lru_kernel(xr_ref, xg_ref, cw_ref, cb_ref, wa_ref, ba_ref, wx_ref, bx_ref, lam_ref,
                g_ref, o_ref, xs_ref, a_ref, b_ref, h_ref):
    ti = pl.program_id(1)
    tt = xr_ref.shape[1]

    @pl.when(ti == 0)
    def _():
        xs_ref[0:SUBLANES, :] = jnp.zeros((SUBLANES, LRU_WIDTH), F32)
        h_ref[...] = jnp.zeros_like(h_ref)

    x = xr_ref[0]
    xs_ref[SUBLANES:SUBLANES + tt, :] = x
    xc = cb_ref[...] + cw_ref[CONV_WIDTH - 1:CONV_WIDTH, :] * x
    for j in range(CONV_WIDTH - 1):
        lag = CONV_WIDTH - 1 - j
        xc = xc + cw_ref[j:j + 1, :] * xs_ref[SUBLANES - lag:SUBLANES - lag + tt, :]
    xs_ref[0:SUBLANES, :] = xs_ref[tt:tt + SUBLANES, :]

    xb = xc.astype(BF16)
    r = jax.nn.sigmoid(jnp.dot(xb, wa_ref[...], preferred_element_type=F32) + ba_ref[...])
    i = jax.nn.sigmoid(jnp.dot(xb, wx_ref[...], preferred_element_type=F32) + bx_ref[...])
    neg_lam = -lam_ref[...]
    softplus = jnp.maximum(neg_lam, 0.0) + jnp.log1p(jnp.exp(-jnp.abs(neg_lam)))
    log_a = -LRU_C * r * softplus
    a = jnp.exp(log_a)
    th = jnp.tanh(log_a)
    mult = jnp.sqrt(jnp.maximum(-2.0 * th / (1.0 - th), 0.0))
    b = mult * i * xc

    sub = lax.broadcasted_iota(jnp.int32, (tt, LRU_WIDTH), 0) & (SUBLANES - 1)
    for s in (1, 2, 4):
        a_prev = pltpu.roll(a, s, axis=0)
        b_prev = pltpu.roll(b, s, axis=0)
        ok = sub >= s
        b = jnp.where(ok, a * b_prev + b, b)
        a = jnp.where(ok, a * a_prev, a)
    a_ref[...] = a
    b_ref[...] = b

    def group_body(k, h):
        r0 = pl.multiple_of(k * SUBLANES, SUBLANES)
        h8 = a_ref[pl.ds(r0, SUBLANES), :] * h + b_ref[pl.ds(r0, SUBLANES), :]
        b_ref[pl.ds(r0, SUBLANES), :] = h8
        return jnp.broadcast_to(h8[SUBLANES - 1:SUBLANES, :], (SUBLANES, LRU_WIDTH))

    h_ref[...] = lax.fori_loop(0, tt // SUBLANES, group_body, h_ref[...])
    out = b_ref[...] * jax.nn.gelu(xg_ref[0])
    o_ref[0] = _rms(out, g_ref[...]).astype(BF16)


def _lru(xr, xg, conv_w, conv_b, wa, ba, wx, bx, lam, g):
    b, t, c = xr.shape
    tt = LRU_TOKENS
    tok = pl.BlockSpec((1, tt, c), lambda bi, ti: (bi, ti, 0))
    vec = _const_spec((1, c))
    return pl.pallas_call(
        _lru_kernel,
        grid=(b, t // tt),
        in_specs=[tok, tok, _const_spec((CONV_WIDTH, c)), vec, _const_spec((c, c)), vec,
                  _const_spec((c, c)), vec, vec, vec],
        out_specs=tok,
        out_shape=jax.ShapeDtypeStruct((b, t, c), BF16),
        scratch_shapes=[pltpu.VMEM((tt + 2 * SUBLANES, c), F32), pltpu.VMEM((tt, c), F32),
                        pltpu.VMEM((tt, c), F32), pltpu.VMEM((SUBLANES, c), F32)],
        compiler_params=_params("parallel", "arbitrary"),
        name="rg_lru",
    )(xr, xg, conv_w, conv_b, wa, ba, wx, bx, lam, g)


def _outproj_kernel(h_ref, attn_ref, lru_ref, ga_ref, post_g_ref, wo_a_ref, wo_l_ref, o_ref):
    ya = _rms(attn_ref[...], ga_ref[...]).astype(BF16)
    m = (jnp.dot(ya, wo_a_ref[...], preferred_element_type=F32)
         + jnp.dot(lru_ref[...], wo_l_ref[...], preferred_element_type=F32))
    o_ref[...] = h_ref[...] + _rms(m, post_g_ref[...])


def _outproj(h, attn, lru, attn_g, post_g, wo_a, wo_l):
    n = h.shape[0]
    tm = PROJ_TOKENS

    def tok(width):
        return pl.BlockSpec((tm, width), lambda i: (i, 0))

    return pl.pallas_call(
        _outproj_kernel,
        grid=(n // tm,),
        in_specs=[tok(D_MODEL), tok(ATTN_WIDTH), tok(LRU_WIDTH), _const_spec((1, ATTN_WIDTH)),
                  _const_spec((1, D_MODEL)), _const_spec((ATTN_WIDTH, D_MODEL)),
                  _const_spec((LRU_WIDTH, D_MODEL))],
        out_specs=tok(D_MODEL),
        out_shape=jax.ShapeDtypeStruct((n, D_MODEL), F32),
        compiler_params=_params("parallel"),
        name="outproj",
    )(h, attn, lru, attn_g, post_g, wo_a, wo_l)


def _pack_w_in(w_in):
    kv_cols = N_KV_GROUPS * HEAD_DIM
    gate_lo = ATTN_WIDTH + 6 * kv_cols
    gate_hi = gate_lo + 3 * N_Q_HEADS
    per_group = 3 * HEADS_PER_GROUP
    pad = jnp.zeros((w_in.shape[0], LANES - per_group), w_in.dtype)
    gate_slabs = []
    for gi in range(N_KV_GROUPS):
        gate_slabs += [w_in[:, gate_lo + gi * per_group:gate_lo + (gi + 1) * per_group], pad]
    return jnp.concatenate([w_in[:, :gate_lo]] + gate_slabs + [w_in[:, gate_hi:]], axis=1).astype(BF16)


def _block_diag(w):
    nb, d, e = w.shape
    eye = jnp.eye(nb, dtype=w.dtype)
    return jnp.einsum("nde,nm->ndme", w, eye).reshape(nb * d, nb * e).astype(BF16)


def _layer(h, p):
    b, t, d = h.shape
    n = b * t
    row = lambda v: v.reshape(1, -1)

    h1 = _ffn(h.reshape(n, d), row(p["ffn1_pre_g"]), row(p["ffn1_post_g"]),
              p["ffn1_w_gate"].astype(BF16), p["ffn1_w_up"].astype(BF16),
              p["ffn1_w_down"].astype(BF16))

    q, cmp_in, kv, gates, xr, xg = _proj(h1.reshape(b, t, d), row(p["mix_pre_g"]),
                                         _pack_w_in(p["w_in"]))
    cmp_kv = _compress(cmp_in, p["cmp_k_pe"], p["cmp_k_w1"].astype(BF16),
                       p["cmp_k_w2"].astype(BF16), p["cmp_v_pe"],
                       p["cmp_v_w1"].astype(BF16), p["cmp_v_w2"].astype(BF16))
    attn = _attention(q, cmp_kv, kv, gates)
    lru = _lru(xr, xg, p["conv_w"], row(p["conv_b"]), _block_diag(p["lru_w_a"]),
               row(p["lru_b_a"]), _block_diag(p["lru_w_x"]), row(p["lru_b_x"]),
               row(p["lru_lambda"]), row(p["lru_out_g"]))
    w_out = p["w_out"].astype(BF16)
    h2 = _outproj(h1, attn.reshape(n, ATTN_WIDTH), lru.reshape(n, LRU_WIDTH),
                  row(p["attn_out_g"]), row(p["mix_post_g"]),
                  w_out[:ATTN_WIDTH], w_out[ATTN_WIDTH:])

    h3 = _ffn(h2, row(p["ffn2_pre_g"]), row(p["ffn2_post_g"]),
              p["ffn2_w_gate"].astype(BF16), p["ffn2_w_up"].astype(BF16),
              p["ffn2_w_down"].astype(BF16))
    return h3.reshape(b, t, d)


_PARAM_NAMES = (
    "ffn1_pre_g", "ffn1_post_g", "ffn1_w_gate", "ffn1_w_up", "ffn1_w_down",
    "mix_pre_g", "mix_post_g", "w_in", "cmp_k_pe", "cmp_k_w1", "cmp_k_w2",
    "cmp_v_pe", "cmp_v_w1", "cmp_v_w2", "conv_w", "conv_b", "lru_w_a", "lru_b_a",
    "lru_w_x", "lru_b_x", "lru_lambda", "attn_out_g", "lru_out_g", "w_out",
    "ffn2_pre_g", "ffn2_post_g", "ffn2_w_gate", "ffn2_w_up", "ffn2_w_down",
)


def kernel(x, ffn1_pre_g, ffn1_post_g, ffn1_w_gate, ffn1_w_up, ffn1_w_down, mix_pre_g, mix_post_g, w_in, cmp_k_pe, cmp_k_w1, cmp_k_w2, cmp_v_pe, cmp_v_w1, cmp_v_w2, conv_w, conv_b, lru_w_a, lru_b_a, lru_w_x, lru_b_x, lru_lambda, attn_out_g, lru_out_g, w_out, ffn2_pre_g, ffn2_post_g, ffn2_w_gate, ffn2_w_up, ffn2_w_down):
    stacked = dict(zip(_PARAM_NAMES, (
        ffn1_pre_g, ffn1_post_g, ffn1_w_gate, ffn1_w_up, ffn1_w_down, mix_pre_g, mix_post_g,
        w_in, cmp_k_pe, cmp_k_w1, cmp_k_w2, cmp_v_pe, cmp_v_w1, cmp_v_w2, conv_w, conv_b,
        lru_w_a, lru_b_a, lru_w_x, lru_b_x, lru_lambda, attn_out_g, lru_out_g, w_out,
        ffn2_pre_g, ffn2_post_g, ffn2_w_gate, ffn2_w_up, ffn2_w_down)))
    h = x
    for layer in range(ffn1_pre_g.shape[0]):
        h = _layer(h, {k: v[layer] for k, v in stacked.items()})
    return h
```

```python
import functools

import numpy as np
import jax
import jax.numpy as jnp
from jax import lax
from jax.experimental import pallas as pl
from jax.experimental.pallas import tpu as pltpu

F32 = jnp.float32
BF16 = jnp.bfloat16

D_MODEL = 1024
N_Q_HEADS = 8
HEAD_DIM = 64
N_KV_GROUPS = 2
HEADS_PER_GROUP = N_Q_HEADS // N_KV_GROUPS
ATTN_WIDTH = N_Q_HEADS * HEAD_DIM
GROUP_WIDTH = HEADS_PER_GROUP * HEAD_DIM
CMP_BLOCK = 32
CMP_STRIDE = 16
CMP_HIDDEN = 256
SEL_BLOCK = 64
SEL_TOPN = 16
WINDOW = 512
Q_BLOCK = 256
LRU_WIDTH = 512
LRU_BLOCKS = 8
CONV_WIDTH = 4
LRU_C = 8.0
D_FF = 2816
NORM_EPS = 1e-6

LANES = 128
SUBLANES = 8
VMEM_LIMIT_BYTES = 56 * 1024 * 1024

NEG_BIG = -1e30
FORCED_SCORE = 3e38
LOG2E = 1.4426950408889634
AUG_DIM = 2 * HEAD_DIM

FFN_TOKENS = 512
FFN_CHUNK = 256
PROJ_TOKENS = 512
LRU_TOKENS = 512
SEL_CHUNK = 256
WIN_KEYS = WINDOW + Q_BLOCK

COL_Q = 0
COL_CMP = COL_Q + ATTN_WIDTH
COL_KV = COL_CMP + 2 * N_KV_GROUPS * HEAD_DIM
COL_GATE = COL_KV + 4 * N_KV_GROUPS * HEAD_DIM
COL_XR = COL_GATE + N_KV_GROUPS * LANES
COL_XG = COL_XR + LRU_WIDTH
PROJ_WIDTH = COL_XG + LRU_WIDTH


def _rms(x, g):
    ms = jnp.mean(x * x, axis=-1, keepdims=True)
    return x * lax.rsqrt(ms + NORM_EPS) * g


def _const_spec(shape):
    nd = len(shape)
    return pl.BlockSpec(shape, lambda *_: (0,) * nd, pipeline_mode=pl.Buffered(1))


def _params(*sem):
    return pltpu.CompilerParams(dimension_semantics=sem, vmem_limit_bytes=VMEM_LIMIT_BYTES)


def _ffn_kernel(x_ref, pre_g_ref, post_g_ref, wg_ref, wu_ref, wd_ref, o_ref, acc_ref):
    x = x_ref[...]
    xb = _rms(x, pre_g_ref[...]).astype(BF16)
    for c in range(D_FF // FFN_CHUNK):
        sl = slice(c * FFN_CHUNK, (c + 1) * FFN_CHUNK)
        gate = jnp.dot(xb, wg_ref[:, sl], preferred_element_type=F32)
        up = jnp.dot(xb, wu_ref[:, sl], preferred_element_type=F32)
        act = (jax.nn.silu(gate) * up).astype(BF16)
        contrib = jnp.dot(act, wd_ref[sl, :], preferred_element_type=F32)
        if c == 0:
            acc_ref[...] = contrib
        else:
            acc_ref[...] += contrib
    o_ref[...] = x + 0.5 * _rms(acc_ref[...], post_g_ref[...])


def _ffn(h, pre_g, post_g, w_gate, w_up, w_down):
    n = h.shape[0]
    tok = pl.BlockSpec((FFN_TOKENS, D_MODEL), lambda i: (i, 0))
    return pl.pallas_call(
        _ffn_kernel,
        grid=(n // FFN_TOKENS,),
        in_specs=[tok, _const_spec((1, D_MODEL)), _const_spec((1, D_MODEL)),
                  _const_spec((D_MODEL, D_FF)), _const_spec((D_MODEL, D_FF)),
                  _const_spec((D_FF, D_MODEL))],
        out_specs=tok,
        out_shape=jax.ShapeDtypeStruct((n, D_MODEL), F32),
        scratch_shapes=[pltpu.VMEM((FFN_TOKENS, D_MODEL), F32)],
        compiler_params=_params("parallel"),
        name="ffn",
    )(h, pre_g, post_g, w_gate, w_up, w_down)


def _key_tail(pos, rows):
    lane = lax.broadcasted_iota(jnp.int32, (rows, HEAD_DIM), 1)
    hi = (pos >> 6).astype(F32)
    lo = (pos & (SEL_BLOCK - 1)).astype(F32)
    return jnp.where(lane < 2, hi, jnp.where(lane < 4, lo, 0.0))


def _value_tail(rows):
    lane = lax.broadcasted_iota(jnp.int32, (rows, HEAD_DIM), 1)
    return jnp.where(lane == 0, 1.0, 0.0)


def _proj_kernel(h_ref, g_ref, w_ref, q_ref, cmp_ref, kv_ref, gate_ref, xr_ref, xg_ref):
    tm = h_ref.shape[1]
    hb = _rms(h_ref[0], g_ref[...]).astype(BF16)
    p = jnp.dot(hb, w_ref[...], preferred_element_type=F32)
    q_ref[0] = (p[:, COL_Q:COL_CMP] * (HEAD_DIM ** -0.5 * LOG2E)).astype(BF16)
    cmp_ref[0] = p[:, COL_CMP:COL_KV]
    pos = pl.program_id(1) * tm + lax.broadcasted_iota(jnp.int32, (tm, 1), 0)
    tails = (_key_tail(pos, tm), _value_tail(tm))
    for i in range(4 * N_KV_GROUPS):
        lo = COL_KV + i * HEAD_DIM
        tail = tails[(i // N_KV_GROUPS) % 2]
        kv_ref[0, i] = jnp.concatenate([p[:, lo:lo + HEAD_DIM], tail], axis=1).astype(BF16)
    gate_ref[0] = jax.nn.sigmoid(p[:, COL_GATE:COL_XR])
    xr_ref[0] = p[:, COL_XR:COL_XG]
    xg_ref[0] = p[:, COL_XG:PROJ_WIDTH]


def _proj(h, g, w_packed):
    b, t, _ = h.shape
    tm = PROJ_TOKENS

    def tok(width):
        return pl.BlockSpec((1, tm, width), lambda bi, ti: (bi, ti, 0))

    return pl.pallas_call(
        _proj_kernel,
        grid=(b, t // tm),
        in_specs=[tok(D_MODEL), _const_spec((1, D_MODEL)), _const_spec((D_MODEL, PROJ_WIDTH))],
        out_specs=[
            tok(ATTN_WIDTH),
            tok(2 * N_KV_GROUPS * HEAD_DIM),
            pl.BlockSpec((1, 4 * N_KV_GROUPS, tm, AUG_DIM), lambda bi, ti: (bi, 0, ti, 0)),
            tok(N_KV_GROUPS * LANES),
            tok(LRU_WIDTH),
            tok(LRU_WIDTH),
        ],
        out_shape=[
            jax.ShapeDtypeStruct((b, t, ATTN_WIDTH), BF16),
            jax.ShapeDtypeStruct((b, t, 2 * N_KV_GROUPS * HEAD_DIM), F32),
            jax.ShapeDtypeStruct((b, 4 * N_KV_GROUPS, t, AUG_DIM), BF16),
            jax.ShapeDtypeStruct((b, t, N_KV_GROUPS * LANES), F32),
            jax.ShapeDtypeStruct((b, t, LRU_WIDTH), F32),
            jax.ShapeDtypeStruct((b, t, LRU_WIDTH), F32),
        ],
        compiler_params=_params("parallel", "parallel"),
        name="proj",
    )(h, g, w_packed)


def _compress_kernel(xk_ref, xv_ref, kpe_ref, kw1_ref, kw2_ref, vpe_ref, vw1_ref, vw2_ref, o_ref):
    n_chunks = xk_ref.shape[1] // CMP_STRIDE
    half = CMP_BLOCK // 2
    kinds = ((xk_ref, kpe_ref, kw1_ref, kw2_ref), (xv_ref, vpe_ref, vw1_ref, vw2_ref))
    top = [jnp.zeros((n_chunks, CMP_HIDDEN), F32) for _ in range(4)]
    bot = [jnp.zeros((n_chunks, CMP_HIDDEN), F32) for _ in range(4)]
    for l in range(half):
        rows = [ref[0, pl.ds(l, n_chunks, stride=CMP_STRIDE), :] for ref in (xk_ref, xv_ref)]
        for s in range(4):
            _, pe_ref, w1_ref, _ = kinds[s // N_KV_GROUPS]
            gi = s % N_KV_GROUPS
            xs = rows[s // N_KV_GROUPS][:, gi * HEAD_DIM:(gi + 1) * HEAD_DIM]
            x_top = (xs + pe_ref[l:l + 1, :]).astype(BF16)
            x_bot = (xs + pe_ref[half + l:half + l + 1, :]).astype(BF16)
            top[s] += jnp.dot(x_top, w1_ref[l * HEAD_DIM:(l + 1) * HEAD_DIM, :],
                              preferred_element_type=F32)
            bot[s] += jnp.dot(x_bot, w1_ref[(half + l) * HEAD_DIM:(half + l + 1) * HEAD_DIM, :],
                              preferred_element_type=F32)
    row = lax.broadcasted_iota(jnp.int32, (n_chunks, HEAD_DIM), 0)
    cmp_end = lax.broadcasted_iota(jnp.int32, (n_chunks, 1), 0) * CMP_STRIDE + (CMP_BLOCK - 1)
    tails = (_key_tail(cmp_end, n_chunks), _value_tail(n_chunks))
    for s in range(4):
        w2_ref = kinds[s // N_KV_GROUPS][3]
        hidden = top[s] + pltpu.roll(bot[s], n_chunks - 1, axis=0)
        out = jnp.dot(jax.nn.gelu(hidden).astype(BF16), w2_ref[...], preferred_element_type=F32)
        out = jnp.where(row < n_chunks - 1, out, 0.0)
        o_ref[0, s] = jnp.concatenate([out, tails[s // N_KV_GROUPS]], axis=1).astype(BF16)


def _compress(cmp_in, k_pe, k_w1, k_w2, v_pe, v_w1, v_w2):
    b, t, _ = cmp_in.shape
    n_chunks = t // CMP_STRIDE
    kv_cols = N_KV_GROUPS * HEAD_DIM
    return pl.pallas_call(
        _compress_kernel,
        grid=(b,),
        in_specs=[pl.BlockSpec((1, t, kv_cols), lambda bi: (bi, 0, 0)),
                  pl.BlockSpec((1, t, kv_cols), lambda bi: (bi, 0, 1)),
                  _const_spec(k_pe.shape), _const_spec(k_w1.shape), _const_spec(k_w2.shape),
                  _const_spec(v_pe.shape), _const_spec(v_w1.shape), _const_spec(v_w2.shape)],
        out_specs=pl.BlockSpec((1, 2 * N_KV_GROUPS, n_chunks, AUG_DIM), lambda bi: (bi, 0, 0, 0)),
        out_shape=jax.ShapeDtypeStruct((b, 2 * N_KV_GROUPS, n_chunks, AUG_DIM), BF16),
        compiler_params=_params("parallel"),
        name="compress",
    )(cmp_in, cmp_in, k_pe, k_w1, k_w2, v_pe, v_w1, v_w2)


_NT = (((1,), (1,)), ((), ()))


def _softmax_numerators(s4, bias):
    n_slabs = s4.shape[1] // LANES
    probs = []
    for r in range(HEADS_PER_GROUP):
        s = s4[r * Q_BLOCK:(r + 1) * Q_BLOCK] + bias
        slabs = [s[:, j * LANES:(j + 1) * LANES] for j in range(n_slabs)]
        lane_max = functools.reduce(jnp.maximum, slabs)
        m = jnp.broadcast_to(jnp.max(lane_max, axis=-1, keepdims=True), (Q_BLOCK, LANES))
        probs.append(jnp.concatenate([jnp.exp2(x - m) for x in slabs], axis=1).astype(BF16))
    return jnp.concatenate(probs, axis=0)


def _attn_kernel(q_ref, slope_ref, kc_ref, vc_ref, ks_ref, vs_ref, kw_ref, vw_ref, gate_ref,
                 mcs_t_ref, expand_ref, chunk_of_ref, o_ref,
                 q4_ref, chosen_ref, s_ref, m_ref, acc_ref, slot_ref):
    qb = pl.program_id(2)
    q0 = qb * Q_BLOCK
    rows = HEADS_PER_GROUP * Q_BLOCK
    n_cmp = kc_ref.shape[2]
    n_sel = expand_ref.shape[1]

    qblk = q_ref[0]
    slope_cols = slope_ref[0]
    q4 = jnp.concatenate(
        [jnp.concatenate([qblk[:, r * HEAD_DIM:(r + 1) * HEAD_DIM],
                          jnp.broadcast_to(slope_cols[r:r + 1, :], (Q_BLOCK, HEAD_DIM))], axis=1)
         for r in range(HEADS_PER_GROUP)], axis=0)
    t_row = q0 + (lax.broadcasted_iota(jnp.int32, (rows, 1), 0) & (Q_BLOCK - 1))
    tq = q0 + lax.broadcasted_iota(jnp.int32, (Q_BLOCK, 1), 0)

    s4 = lax.dot_general(q4, kc_ref[0, 0], _NT, preferred_element_type=F32)
    cmp_end = lax.broadcasted_iota(jnp.int32, (1, n_cmp), 1) * CMP_STRIDE + (CMP_BLOCK - 1)
    pc4 = _softmax_numerators(s4, jnp.where(tq >= cmp_end, 0.0, NEG_BIG))
    ov = jnp.dot(pc4, vc_ref[0, 0], preferred_element_type=F32)
    o_cmp = jnp.where(t_row >= CMP_BLOCK - 1, ov[:, :HEAD_DIM] / ov[:, HEAD_DIM:HEAD_DIM + 1], 0.0)

    w0 = pl.multiple_of(jnp.maximum(q0 - WINDOW, 0), Q_BLOCK)
    kwin = kw_ref[0, 0, pl.ds(w0, WIN_KEYS), :]
    vwin = vw_ref[0, 0, pl.ds(w0, WIN_KEYS), :]
    s4w = lax.dot_general(q4, kwin, _NT, preferred_element_type=F32)
    d = tq - (w0 + lax.broadcasted_iota(jnp.int32, (1, WIN_KEYS), 1))
    bias_w = jnp.where(d >= 0, jnp.where(d < WINDOW, 0.0, NEG_BIG), NEG_BIG)
    wv = jnp.dot(_softmax_numerators(s4w, bias_w), vwin, preferred_element_type=F32)
    o_win = wv[:, :HEAD_DIM] / wv[:, HEAD_DIM:HEAD_DIM + 1]

    imp_l = lax.dot_general(mcs_t_ref[...], pc4, _NT, preferred_element_type=F32)
    imp = None
    for r in range(HEADS_PER_GROUP):
        cs = slice(r * Q_BLOCK, (r + 1) * Q_BLOCK)
        part = imp_l[:n_sel, cs] / imp_l[n_sel:n_sel + 1, cs]
        imp = part if imp is None else imp + part
    blk = lax.broadcasted_iota(jnp.int32, (n_sel, Q_BLOCK), 0)
    tq_l = q0 + lax.broadcasted_iota(jnp.int32, (n_sel, Q_BLOCK), 1)
    cur = tq_l >> 6
    forced = (blk == 0) | (blk == cur) | (blk == cur - 1)
    valid = blk * SEL_BLOCK <= tq_l
    score = jnp.where(forced, FORCED_SCORE, jnp.where(valid, imp, -1.0))
    n_slabs = n_sel // SUBLANES
    slabs = [score[s * SUBLANES:(s + 1) * SUBLANES, :] for s in range(n_slabs)]
    ranks = [jnp.zeros((SUBLANES, Q_BLOCK), F32) for _ in range(n_slabs)]
    sub = lax.broadcasted_iota(jnp.int32, (SUBLANES, Q_BLOCK), 0)
    for j in range(n_sel):
        other = jnp.broadcast_to(score[j:j + 1, :], (SUBLANES, Q_BLOCK))
        for s in range(n_slabs):
            if s * SUBLANES > j:
                ahead = jnp.where(other >= slabs[s], 1.0, 0.0)
            elif (s + 1) * SUBLANES - 1 <= j:
                ahead = jnp.where(other > slabs[s], 1.0, 0.0)
            else:
                ahead = jnp.where(sub > j - s * SUBLANES, jnp.where(other >= slabs[s], 1.0, 0.0),
                                  jnp.where(other > slabs[s], 1.0, 0.0))
            ranks[s] = ranks[s] + ahead
    rank = jnp.concatenate(ranks, axis=0)
    chosen_t = jnp.where(valid, jnp.where(rank < float(min(SEL_TOPN, n_sel)), 1.0, 0.0), 0.0)
    chosen_ref[...] = chosen_t.T.astype(BF16)
    q4_ref[...] = q4

    n_chunks = expand_ref.shape[0]
    per_chunk = jnp.dot(chunk_of_ref[...], chosen_t.astype(BF16), preferred_element_type=F32)
    live = jnp.max(per_chunk, axis=1, keepdims=True) > 0.0
    weight = (1 << lax.broadcasted_iota(jnp.int32, (n_chunks, 1), 0)).astype(F32)
    live_bits = jnp.sum(jnp.where(live, weight, 0.0)).astype(jnp.int32)

    n_live = 0
    for c in range(n_chunks):
        slot_ref[n_live] = c
        n_live = n_live + ((live_bits >> c) & 1)
    slot_ref[n_live] = 0
    n_pairs = (n_live + 1) // 2

    m_ref[...] = jnp.full(m_ref.shape, NEG_BIG, F32)

    def score_body(i, carry):
        lane_max = [None] * HEADS_PER_GROUP
        for half in range(2):
            slot = 2 * i + half
            c = slot_ref[slot]
            real = slot < n_live
            k0 = pl.multiple_of(c * SEL_CHUNK, SEL_CHUNK)
            k = ks_ref[0, 0, pl.ds(k0, SEL_CHUNK), :]
            s4c = lax.dot_general(q4_ref[...], k, _NT, preferred_element_type=F32)
            hit = jnp.dot(chosen_ref[...], expand_ref[c], preferred_element_type=F32)
            pos = k0 + lax.broadcasted_iota(jnp.int32, (1, SEL_CHUNK), 1)
            bias = jnp.where(pos <= tq, (hit - 1.0) * (-NEG_BIG), NEG_BIG)
            bias = jnp.where(real, bias, NEG_BIG)
            for r in range(HEADS_PER_GROUP):
                rs = slice(r * Q_BLOCK, (r + 1) * Q_BLOCK)
                s = s4c[rs] + bias
                s_ref[slot, rs, :] = s
                mx = jnp.maximum(s[:, :LANES], s[:, LANES:])
                lane_max[r] = mx if lane_max[r] is None else jnp.maximum(lane_max[r], mx)
        for r in range(HEADS_PER_GROUP):
            rs = slice(r * Q_BLOCK, (r + 1) * Q_BLOCK)
            m_ref[rs, :] = jnp.maximum(m_ref[rs, :], lane_max[r])
        return carry

    lax.fori_loop(0, n_pairs, score_body, 0)
    m_ref[...] = jnp.broadcast_to(jnp.max(m_ref[...], axis=-1, keepdims=True), m_ref.shape)

    acc_ref[...] = jnp.zeros(acc_ref.shape, F32)

    def value_body(i, carry):
        pv = None
        for half in range(2):
            slot = 2 * i + half
            v0 = pl.multiple_of(slot_ref[slot] * SEL_CHUNK, SEL_CHUNK)
            v = vs_ref[0, 0, pl.ds(v0, SEL_CHUNK), :]
            p = jnp.concatenate([jnp.exp2(s_ref[slot, :, :LANES] - m_ref[...]),
                                 jnp.exp2(s_ref[slot, :, LANES:] - m_ref[...])],
                                axis=1).astype(BF16)
            d = jnp.dot(p, v, preferred_element_type=F32)
            pv = d if pv is None else pv + d
        acc_ref[...] += pv
        return carry

    lax.fori_loop(0, n_pairs, value_body, 0)
    acc_sel = acc_ref[...]
    o_sel = acc_sel[:, :HEAD_DIM] / acc_sel[:, HEAD_DIM:HEAD_DIM + 1]

    gates = gate_ref[0]
    outs = []
    for r in range(HEADS_PER_GROUP):
        rs = slice(r * Q_BLOCK, (r + 1) * Q_BLOCK)
        outs.append(gates[:, 3 * r:3 * r + 1] * o_cmp[rs]
                    + gates[:, 3 * r + 1:3 * r + 2] * o_sel[rs]
                    + gates[:, 3 * r + 2:3 * r + 3] * o_win[rs])
    o_ref[0] = jnp.concatenate(outs, axis=-1)


def _attention(q, cmp_kv, kv, gates):
    b, t, _ = q.shape
    n_cmp = cmp_kv.shape[2]
    n_sel = t // SEL_BLOCK
    n_real_cmp = (t - CMP_BLOCK) // CMP_STRIDE + 1

    cs = np.arange(n_cmp)[None, :] * CMP_STRIDE
    js = np.arange(n_sel)[:, None] * SEL_BLOCK
    ov = np.minimum(cs + CMP_BLOCK, js + SEL_BLOCK) - np.maximum(cs, js)
    mcs_t = np.zeros((n_sel + SUBLANES, n_cmp), np.float32)
    mcs_t[:n_sel] = np.maximum(ov, 0).astype(np.float32) / CMP_BLOCK
    mcs_t[:n_sel, n_real_cmp:] = 0.0
    mcs_t[n_sel] = 1.0
    key_blk = (np.arange(t) // SEL_BLOCK).reshape(t // SEL_CHUNK, 1, SEL_CHUNK)
    expand = (key_blk == np.arange(n_sel)[None, :, None]).astype(np.float32)
    n_chunks = t // SEL_CHUNK
    chunk_of = (np.arange(n_sel)[None, :] // (SEL_CHUNK // SEL_BLOCK)
                == np.arange(n_chunks)[:, None]).astype(np.float32)
    rows = HEADS_PER_GROUP * Q_BLOCK

    slope_l2 = LOG2E * np.power(2.0, -8.0 * np.arange(1, N_Q_HEADS + 1) / N_Q_HEADS)
    hi = slope_l2.astype(BF16).astype(np.float64)
    lo = (slope_l2 - hi).astype(BF16).astype(np.float64)
    cols = np.zeros((N_KV_GROUPS, SUBLANES, HEAD_DIM), np.float32)
    heads = cols[:, :HEADS_PER_GROUP].reshape(N_Q_HEADS, HEAD_DIM)
    heads[:, 0], heads[:, 1], heads[:, 2], heads[:, 3] = SEL_BLOCK * hi, SEL_BLOCK * lo, hi, lo
    cols[:, :HEADS_PER_GROUP] = heads.reshape(N_KV_GROUPS, HEADS_PER_GROUP, HEAD_DIM)

    def kv_spec(kind, rows):
        return pl.BlockSpec((1, 1, rows, AUG_DIM),
                            lambda bi, gi, qi, kind=kind: (bi, kind * N_KV_GROUPS + gi, 0, 0))

    return pl.pallas_call(
        _attn_kernel,
        grid=(b, N_KV_GROUPS, t // Q_BLOCK),
        in_specs=[
            pl.BlockSpec((1, Q_BLOCK, GROUP_WIDTH), lambda bi, gi, qi: (bi, qi, gi)),
            pl.BlockSpec((1, SUBLANES, HEAD_DIM), lambda bi, gi, qi: (gi, 0, 0)),
            kv_spec(0, n_cmp), kv_spec(1, n_cmp),
            kv_spec(0, t), kv_spec(1, t), kv_spec(2, t), kv_spec(3, t),
            pl.BlockSpec((1, Q_BLOCK, LANES), lambda bi, gi, qi: (bi, qi, gi)),
            _const_spec(mcs_t.shape), _const_spec(expand.shape), _const_spec(chunk_of.shape),
        ],
        out_specs=pl.BlockSpec((1, Q_BLOCK, GROUP_WIDTH), lambda bi, gi, qi: (bi, qi, gi)),
        out_shape=jax.ShapeDtypeStruct((b, t, ATTN_WIDTH), F32),
        scratch_shapes=[
            pltpu.VMEM((rows, AUG_DIM), BF16),
            pltpu.VMEM((Q_BLOCK, n_sel), BF16),
            pltpu.VMEM((n_chunks, rows, SEL_CHUNK), F32),
            pltpu.VMEM((rows, LANES), F32),
            pltpu.VMEM((rows, AUG_DIM), F32),
            pltpu.SMEM((n_chunks + 1,), jnp.int32),
        ],
        compiler_params=_params("parallel", "parallel", "arbitrary"),
        name="nsa_attn",
    )(q, jnp.asarray(cols, BF16), cmp_kv, cmp_kv, kv, kv, kv, kv, gates,
      jnp.asarray(mcs_t, BF16), jnp.asarray(expand, BF16), jnp.asarray(chunk_of, BF16))


def _lru_kernel(xr_ref, xg_ref, cw_ref, cb_ref, wa_ref, ba_ref, wx_ref, bx_ref, lam_ref,
                g_ref, o_ref, xs_ref, a_ref, b_ref, h_ref):
    ti = pl.program_id(1)
    tt = xr_ref.shape[1]

    @pl.when(ti == 0)
    def _():
        xs_ref[0:SUBLANES, :] = jnp.zeros((SUBLANES, LRU_WIDTH), F32)
        h_ref[...] = jnp.zeros_like(h_ref)

    x = xr_ref[0]
    xs_ref[SUBLANES:SUBLANES + tt, :] = x
    xc = cb_ref[...] + cw_ref[CONV_WIDTH - 1:CONV_WIDTH, :] * x
    for j in range(CONV_WIDTH - 1):
        lag = CONV_WIDTH - 1 - j
        xc = xc + cw_ref[j:j + 1, :] * xs_ref[SUBLANES - lag:SUBLANES - lag + tt, :]
    xs_ref[0:SUBLANES, :] = xs_ref[tt:tt + SUBLANES, :]

    xb = xc.astype(BF16)
    r = jax.nn.sigmoid(jnp.dot(xb, wa_ref[...], preferred_element_type=F32) + ba_ref[...])
    i = jax.nn.sigmoid(jnp.dot(xb, wx_ref[...], preferred_element_type=F32) + bx_ref[...])
    neg_lam = -lam_ref[...]
    softplus = jnp.maximum(neg_lam, 0.0) + jnp.log1p(jnp.exp(-jnp.abs(neg_lam)))
    log_a = -LRU_C * r * softplus
    a = jnp.exp(log_a)
    th = jnp.tanh(log_a)
    mult = jnp.sqrt(jnp.maximum(-2.0 * th / (1.0 - th), 0.0))
    b = mult * i * xc

    sub = lax.broadcasted_iota(jnp.int32, (tt, LRU_WIDTH), 0) & (SUBLANES - 1)
    for s in (1, 2, 4):
        a_prev = pltpu.roll(a, s, axis=0)
        b_prev = pltpu.roll(b, s, axis=0)
        ok = sub >= s
        b = jnp.where(ok, a * b_prev + b, b)
        a = jnp.where(ok, a * a_prev, a)
    a_ref[...] = a
    b_ref[...] = b

    def group_body(k, h):
        r0 = pl.multiple_of(k * SUBLANES, SUBLANES)
        h8 = a_ref[pl.ds(r0, SUBLANES), :] * h + b_ref[pl.ds(r0, SUBLANES), :]
        b_ref[pl.ds(r0, SUBLANES), :] = h8
        return jnp.broadcast_to(h8[SUBLANES - 1:SUBLANES, :], (SUBLANES, LRU_WIDTH))

    h_ref[...] = lax.fori_loop(0, tt // SUBLANES, group_body, h_ref[...])
    out = b_ref[...] * jax.nn.gelu(xg_ref[0])
    o_ref[0] = _rms(out, g_ref[...]).astype(BF16)


def _lru(xr, xg, conv_w, conv_b, wa, ba, wx, bx, lam, g):
    b, t, c = xr.shape
    tt = LRU_TOKENS
    tok = pl.BlockSpec((1, tt, c), lambda bi, ti: (bi, ti, 0))
    vec = _const_spec((1, c))
    return pl.pallas_call(
        _lru_kernel,
        grid=(b, t // tt),
        in_specs=[tok, tok, _const_spec((CONV_WIDTH, c)), vec, _const_spec((c, c)), vec,
                  _const_spec((c, c)), vec, vec, vec],
        out_specs=tok,
        out_shape=jax.ShapeDtypeStruct((b, t, c), BF16),
        scratch_shapes=[pltpu.VMEM((tt + 2 * SUBLANES, c), F32), pltpu.VMEM((tt, c), F32),
                        pltpu.VMEM((tt, c), F32), pltpu.VMEM((SUBLANES, c), F32)],
        compiler_params=_params("parallel", "arbitrary"),
        name="rg_lru",
    )(xr, xg, conv_w, conv_b, wa, ba, wx, bx, lam, g)


def _outproj_kernel(h_ref, attn_ref, lru_ref, ga_ref, post_g_ref, wo_a_ref, wo_l_ref, o_ref):
    ya = _rms(attn_ref[...], ga_ref[...]).astype(BF16)
    m = (jnp.dot(ya, wo_a_ref[...], preferred_element_type=F32)
         + jnp.dot(lru_ref[...], wo_l_ref[...], preferred_element_type=F32))
    o_ref[...] = h_ref[...] + _rms(m, post_g_ref[...])


def _outproj(h, attn, lru, attn_g, post_g, wo_a, wo_l):
    n = h.shape[0]
    tm = PROJ_TOKENS

    def tok(width):
        return pl.BlockSpec((tm, width), lambda i: (i, 0))

    return pl.pallas_call(
        _outproj_kernel,
        grid=(n // tm,),
        in_specs=[tok(D_MODEL), tok(ATTN_WIDTH), tok(LRU_WIDTH), _const_spec((1, ATTN_WIDTH)),
                  _const_spec((1, D_MODEL)), _const_spec((ATTN_WIDTH, D_MODEL)),
                  _const_spec((LRU_WIDTH, D_MODEL))],
        out_specs=tok(D_MODEL),
        out_shape=jax.ShapeDtypeStruct((n, D_MODEL), F32),
        compiler_params=_params("parallel"),
        name="outproj",
    )(h, attn, lru, attn_g, post_g, wo_a, wo_l)


def _pack_w_in(w_in):
    kv_cols = N_KV_GROUPS * HEAD_DIM
    gate_lo = ATTN_WIDTH + 6 * kv_cols
    gate_hi = gate_lo + 3 * N_Q_HEADS
    per_group = 3 * HEADS_PER_GROUP
    pad = jnp.zeros((w_in.shape[0], LANES - per_group), w_in.dtype)
    gate_slabs = []
    for gi in range(N_KV_GROUPS):
        gate_slabs += [w_in[:, gate_lo + gi * per_group:gate_lo + (gi + 1) * per_group], pad]
    return jnp.concatenate([w_in[:, :gate_lo]] + gate_slabs + [w_in[:, gate_hi:]], axis=1).astype(BF16)


def _block_diag(w):
    nb, d, e = w.shape
    eye = jnp.eye(nb, dtype=w.dtype)
    return jnp.einsum("nde,nm->ndme", w, eye).reshape(nb * d, nb * e).astype(BF16)


def _layer(h, p):
    b, t, d = h.shape
    n = b * t
    row = lambda v: v.reshape(1, -1)

    h1 = _ffn(h.reshape(n, d), row(p["ffn1_pre_g"]), row(p["ffn1_post_g"]),
              p["ffn1_w_gate"].astype(BF16), p["ffn1_w_up"].astype(BF16),
              p["ffn1_w_down"].astype(BF16))

    q, cmp_in, kv, gates, xr, xg = _proj(h1.reshape(b, t, d), row(p["mix_pre_g"]),
                                         _pack_w_in(p["w_in"]))
    cmp_kv = _compress(cmp_in, p["cmp_k_pe"], p["cmp_k_w1"].astype(BF16),
                       p["cmp_k_w2"].astype(BF16), p["cmp_v_pe"],
                       p["cmp_v_w1"].astype(BF16), p["cmp_v_w2"].astype(BF16))
    attn = _attention(q, cmp_kv, kv, gates)
    lru = _lru(xr, xg, p["conv_w"], row(p["conv_b"]), _block_diag(p["lru_w_a"]),
               row(p["lru_b_a"]), _block_diag(p["lru_w_x"]), row(p["lru_b_x"]),
               row(p["lru_lambda"]), row(p["lru_out_g"]))
    w_out = p["w_out"].astype(BF16)
    h2 = _outproj(h1, attn.reshape(n, ATTN_WIDTH), lru.reshape(n, LRU_WIDTH),
                  row(p["attn_out_g"]), row(p["mix_post_g"]),
                  w_out[:ATTN_WIDTH], w_out[ATTN_WIDTH:])

    h3 = _ffn(h2, row(p["ffn2_pre_g"]), row(p["ffn2_post_g"]),
              p["ffn2_w_gate"].astype(BF16), p["ffn2_w_up"].astype(BF16),
              p["ffn2_w_down"].astype(BF16))
    return h3.reshape(b, t, d)


_PARAM_NAMES = (
    "ffn1_pre_g", "ffn1_post_g", "ffn1_w_gate", "ffn1_w_up", "ffn1_w_down",
    "mix_pre_g", "mix_post_g", "w_in", "cmp_k_pe", "cmp_k_w1", "cmp_k_w2",
    "cmp_v_pe", "cmp_v_w1", "cmp_v_w2", "conv_w", "conv_b", "lru_w_a", "lru_b_a",
    "lru_w_x", "lru_b_x", "lru_lambda", "attn_out_g", "lru_out_g", "w_out",
    "ffn2_pre_g", "ffn2_post_g", "ffn2_w_gate", "ffn2_w_up", "ffn2_w_down",
)


def kernel(x, ffn1_pre_g, ffn1_post_g, ffn1_w_gate, ffn1_w_up, ffn1_w_down, mix_pre_g, mix_post_g, w_in, cmp_k_pe, cmp_k_w1, cmp_k_w2, cmp_v_pe, cmp_v_w1, cmp_v_w2, conv_w, conv_b, lru_w_a, lru_b_a, lru_w_x, lru_b_x, lru_lambda, attn_out_g, lru_out_g, w_out, ffn2_pre_g, ffn2_post_g, ffn2_w_gate, ffn2_w_up, ffn2_w_down):
    stacked = dict(zip(_PARAM_NAMES, (
        ffn1_pre_g, ffn1_post_g, ffn1_w_gate, ffn1_w_up, ffn1_w_down, mix_pre_g, mix_post_g,
        w_in, cmp_k_pe, cmp_k_w1, cmp_k_w2, cmp_v_pe, cmp_v_w1, cmp_v_w2, conv_w, conv_b,
        lru_w_a, lru_b_a, lru_w_x, lru_b_x, lru_lambda, attn_out_g, lru_out_g, w_out,
        ffn2_pre_g, ffn2_post_g, ffn2_w_gate, ffn2_w_up, ffn2_w_down)))
    h = x
    for layer in range(ffn1_pre_g.shape[0]):
        h = _layer(h, {k: v[layer] for k, v in stacked.items()})
    return h
```

```python
import functools

import numpy as np
import jax
import jax.numpy as jnp
from jax import lax
from jax.experimental import pallas as pl
from jax.experimental.pallas import tpu as pltpu

F32 = jnp.float32
BF16 = jnp.bfloat16

D_MODEL = 1024
N_Q_HEADS = 8
HEAD_DIM = 64
N_KV_GROUPS = 2
HEADS_PER_GROUP = N_Q_HEADS // N_KV_GROUPS
ATTN_WIDTH = N_Q_HEADS * HEAD_DIM
GROUP_WIDTH = HEADS_PER_GROUP * HEAD_DIM
CMP_BLOCK = 32
CMP_STRIDE = 16
CMP_HIDDEN = 256
SEL_BLOCK = 64
SEL_TOPN = 16
WINDOW = 512
Q_BLOCK = 256
LRU_WIDTH = 512
LRU_BLOCKS = 8
CONV_WIDTH = 4
LRU_C = 8.0
D_FF = 2816
NORM_EPS = 1e-6

LANES = 128
SUBLANES = 8
VMEM_LIMIT_BYTES = 56 * 1024 * 1024

NEG_BIG = -1e30
FORCED_SCORE = 3e38
LOG2E = 1.4426950408889634
AUG_DIM = 2 * HEAD_DIM
WIDE_DIM = 4 * HEAD_DIM

FFN_TOKENS = 512
FFN_CHUNK = 256
PROJ_TOKENS = 512
LRU_TOKENS = 512
SEL_CHUNK = 256
WIN_KEYS = WINDOW + Q_BLOCK

COL_Q = 0
COL_CMP = COL_Q + ATTN_WIDTH
COL_KV = COL_CMP + 2 * N_KV_GROUPS * HEAD_DIM
COL_GATE = COL_KV + 4 * N_KV_GROUPS * HEAD_DIM
COL_XR = COL_GATE + N_KV_GROUPS * LANES
COL_XG = COL_XR + LRU_WIDTH
PROJ_WIDTH = COL_XG + LRU_WIDTH


def _rms(x, g):
    ms = jnp.mean(x * x, axis=-1, keepdims=True)
    return x * lax.rsqrt(ms + NORM_EPS) * g


def _const_spec(shape):
    nd = len(shape)
    return pl.BlockSpec(shape, lambda *_: (0,) * nd, pipeline_mode=pl.Buffered(1))


def _params(*sem):
    return pltpu.CompilerParams(dimension_semantics=sem, vmem_limit_bytes=VMEM_LIMIT_BYTES)


def _ffn_kernel(x_ref, pre_g_ref, post_g_ref, wg_ref, wu_ref, wd_ref, o_ref, acc_ref):
    x = x_ref[...]
    xb = _rms(x, pre_g_ref[...]).astype(BF16)
    for c in range(D_FF // FFN_CHUNK):
        sl = slice(c * FFN_CHUNK, (c + 1) * FFN_CHUNK)
        gate = jnp.dot(xb, wg_ref[:, sl], preferred_element_type=F32)
        up = jnp.dot(xb, wu_ref[:, sl], preferred_element_type=F32)
        act = (jax.nn.silu(gate) * up).astype(BF16)
        contrib = jnp.dot(act, wd_ref[sl, :], preferred_element_type=F32)
        if c == 0:
            acc_ref[...] = contrib
        else:
            acc_ref[...] += contrib
    o_ref[...] = x + 0.5 * _rms(acc_ref[...], post_g_ref[...])


def _ffn(h, pre_g, post_g, w_gate, w_up, w_down):
    n = h.shape[0]
    tok = pl.BlockSpec((FFN_TOKENS, D_MODEL), lambda i: (i, 0))
    return pl.pallas_call(
        _ffn_kernel,
        grid=(n // FFN_TOKENS,),
        in_specs=[tok, _const_spec((1, D_MODEL)), _const_spec((1, D_MODEL)),
                  _const_spec((D_MODEL, D_FF)), _const_spec((D_MODEL, D_FF)),
                  _const_spec((D_FF, D_MODEL))],
        out_specs=tok,
        out_shape=jax.ShapeDtypeStruct((n, D_MODEL), F32),
        scratch_shapes=[pltpu.VMEM((FFN_TOKENS, D_MODEL), F32)],
        compiler_params=_params("parallel"),
        name="ffn",
    )(h, pre_g, post_g, w_gate, w_up, w_down)


def _key_tail(pos, rows):
    lane = lax.broadcasted_iota(jnp.int32, (rows, HEAD_DIM), 1)
    hi = (pos >> 6).astype(F32)
    lo = (pos & (SEL_BLOCK - 1)).astype(F32)
    return jnp.where(lane < 2, hi, jnp.where(lane < 4, lo, 0.0))


def _wide_value(v):
    return jnp.concatenate([v, v, jnp.ones((v.shape[0], LANES), F32)], axis=1).astype(BF16)


def _proj_kernel(h_ref, g_ref, w_ref, q_ref, cmp_ref, k_ref, v_ref, gate_ref, xr_ref, xg_ref):
    tm = h_ref.shape[1]
    hb = _rms(h_ref[0], g_ref[...]).astype(BF16)
    p = jnp.dot(hb, w_ref[...], preferred_element_type=F32)
    q_ref[0] = (p[:, COL_Q:COL_CMP] * (HEAD_DIM ** -0.5 * LOG2E)).astype(BF16)
    cmp_ref[0] = p[:, COL_CMP:COL_KV]
    pos = pl.program_id(1) * tm + lax.broadcasted_iota(jnp.int32, (tm, 1), 0)
    key_tail = _key_tail(pos, tm)
    for i in range(4 * N_KV_GROUPS):
        lo = COL_KV + i * HEAD_DIM
        x = p[:, lo:lo + HEAD_DIM]
        branch, is_value, gi = i // (2 * N_KV_GROUPS), (i // N_KV_GROUPS) % 2, i % N_KV_GROUPS
        if is_value:
            v_ref[0, branch * N_KV_GROUPS + gi] = _wide_value(x)
        else:
            k_ref[0, branch * N_KV_GROUPS + gi] = jnp.concatenate([x, key_tail], axis=1).astype(BF16)
    gate_ref[0] = jax.nn.sigmoid(p[:, COL_GATE:COL_XR])
    xr_ref[0] = p[:, COL_XR:COL_XG]
    xg_ref[0] = p[:, COL_XG:PROJ_WIDTH]


def _proj(h, g, w_packed):
    b, t, _ = h.shape
    tm = PROJ_TOKENS

    def tok(width):
        return pl.BlockSpec((1, tm, width), lambda bi, ti: (bi, ti, 0))

    return pl.pallas_call(
        _proj_kernel,
        grid=(b, t // tm),
        in_specs=[tok(D_MODEL), _const_spec((1, D_MODEL)), _const_spec((D_MODEL, PROJ_WIDTH))],
        out_specs=[
            tok(ATTN_WIDTH),
            tok(2 * N_KV_GROUPS * HEAD_DIM),
            pl.BlockSpec((1, 2 * N_KV_GROUPS, tm, AUG_DIM), lambda bi, ti: (bi, 0, ti, 0)),
            pl.BlockSpec((1, 2 * N_KV_GROUPS, tm, WIDE_DIM), lambda bi, ti: (bi, 0, ti, 0)),
            tok(N_KV_GROUPS * LANES),
            tok(LRU_WIDTH),
            tok(LRU_WIDTH),
        ],
        out_shape=[
            jax.ShapeDtypeStruct((b, t, ATTN_WIDTH), BF16),
            jax.ShapeDtypeStruct((b, t, 2 * N_KV_GROUPS * HEAD_DIM), F32),
            jax.ShapeDtypeStruct((b, 2 * N_KV_GROUPS, t, AUG_DIM), BF16),
            jax.ShapeDtypeStruct((b, 2 * N_KV_GROUPS, t, WIDE_DIM), BF16),
            jax.ShapeDtypeStruct((b, t, N_KV_GROUPS * LANES), F32),
            jax.ShapeDtypeStruct((b, t, LRU_WIDTH), F32),
            jax.ShapeDtypeStruct((b, t, LRU_WIDTH), F32),
        ],
        compiler_params=_params("parallel", "parallel"),
        name="proj",
    )(h, g, w_packed)


def _compress_kernel(xk_ref, xv_ref, kpe_ref, kw1_ref, kw2_ref, vpe_ref, vw1_ref, vw2_ref,
                     ok_ref, ov_ref):
    n_chunks = xk_ref.shape[1] // CMP_STRIDE
    half = CMP_BLOCK // 2
    kinds = ((xk_ref, kpe_ref, kw1_ref, kw2_ref), (xv_ref, vpe_ref, vw1_ref, vw2_ref))
    top = [jnp.zeros((n_chunks, CMP_HIDDEN), F32) for _ in range(4)]
    bot = [jnp.zeros((n_chunks, CMP_HIDDEN), F32) for _ in range(4)]
    for l in range(half):
        rows = [ref[0, pl.ds(l, n_chunks, stride=CMP_STRIDE), :] for ref in (xk_ref, xv_ref)]
        for s in range(4):
            _, pe_ref, w1_ref, _ = kinds[s // N_KV_GROUPS]
            gi = s % N_KV_GROUPS
            xs = rows[s // N_KV_GROUPS][:, gi * HEAD_DIM:(gi + 1) * HEAD_DIM]
            x_top = (xs + pe_ref[l:l + 1, :]).astype(BF16)
            x_bot = (xs + pe_ref[half + l:half + l + 1, :]).astype(BF16)
            top[s] += jnp.dot(x_top, w1_ref[l * HEAD_DIM:(l + 1) * HEAD_DIM, :],
                              preferred_element_type=F32)
            bot[s] += jnp.dot(x_bot, w1_ref[(half + l) * HEAD_DIM:(half + l + 1) * HEAD_DIM, :],
                              preferred_element_type=F32)
    row = lax.broadcasted_iota(jnp.int32, (n_chunks, HEAD_DIM), 0)
    cmp_end = lax.broadcasted_iota(jnp.int32, (n_chunks, 1), 0) * CMP_STRIDE + (CMP_BLOCK - 1)
    key_tail = _key_tail(cmp_end, n_chunks)
    for s in range(4):
        w2_ref = kinds[s // N_KV_GROUPS][3]
        hidden = top[s] + pltpu.roll(bot[s], n_chunks - 1, axis=0)
        out = jnp.dot(jax.nn.gelu(hidden).astype(BF16), w2_ref[...], preferred_element_type=F32)
        out = jnp.where(row < n_chunks - 1, out, 0.0)
        if s // N_KV_GROUPS == 0:
            ok_ref[0, s % N_KV_GROUPS] = jnp.concatenate([out, key_tail], axis=1).astype(BF16)
        else:
            ov_ref[0, s % N_KV_GROUPS] = _wide_value(out)


def _compress(cmp_in, k_pe, k_w1, k_w2, v_pe, v_w1, v_w2):
    b, t, _ = cmp_in.shape
    n_chunks = t // CMP_STRIDE
    kv_cols = N_KV_GROUPS * HEAD_DIM
    return pl.pallas_call(
        _compress_kernel,
        grid=(b,),
        in_specs=[pl.BlockSpec((1, t, kv_cols), lambda bi: (bi, 0, 0)),
                  pl.BlockSpec((1, t, kv_cols), lambda bi: (bi, 0, 1)),
                  _const_spec(k_pe.shape), _const_spec(k_w1.shape), _const_spec(k_w2.shape),
                  _const_spec(v_pe.shape), _const_spec(v_w1.shape), _const_spec(v_w2.shape)],
        out_specs=[pl.BlockSpec((1, N_KV_GROUPS, n_chunks, AUG_DIM), lambda bi: (bi, 0, 0, 0)),
                   pl.BlockSpec((1, N_KV_GROUPS, n_chunks, WIDE_DIM), lambda bi: (bi, 0, 0, 0))],
        out_shape=[jax.ShapeDtypeStruct((b, N_KV_GROUPS, n_chunks, AUG_DIM), BF16),
                   jax.ShapeDtypeStruct((b, N_KV_GROUPS, n_chunks, WIDE_DIM), BF16)],
        compiler_params=_params("parallel"),
        name="compress",
    )(cmp_in, cmp_in, k_pe, k_w1, k_w2, v_pe, v_w1, v_w2)


_NT = (((1,), (1,)), ((), ()))


def _softmax_numerators(s, bias):
    n_slabs = s.shape[1] // LANES
    probs = []
    for r in range(s.shape[0] // Q_BLOCK):
        sb = s[r * Q_BLOCK:(r + 1) * Q_BLOCK] + bias
        slabs = [sb[:, j * LANES:(j + 1) * LANES] for j in range(n_slabs)]
        lane_max = functools.reduce(jnp.maximum, slabs)
        m = jnp.broadcast_to(jnp.max(lane_max, axis=-1, keepdims=True), (Q_BLOCK, LANES))
        probs.append(jnp.concatenate([jnp.exp2(x - m) for x in slabs], axis=1).astype(BF16))
    return jnp.concatenate(probs, axis=0)


def _dense_branch(q4, k, v, bias):
    half = q4.shape[0] // 2
    probs, outs = [], []
    for i in range(2):
        s = lax.dot_general(q4[i * half:(i + 1) * half], k, _NT, preferred_element_type=F32)
        p = _softmax_numerators(s, bias)
        probs.append(p)
        outs.append(jnp.dot(p, v, preferred_element_type=F32))
    return jnp.concatenate(probs, axis=0), jnp.concatenate(outs, axis=0)


def _attn_kernel(q_ref, slope_ref, kc_ref, vc_ref, ks_ref, vs_ref, kw_ref, vw_ref, gate_ref,
                 mcs_t_ref, expand_ref, chunk_of_ref, o_ref,
                 q4_ref, chosen_ref, s_ref, m_ref, acc_ref, gate_b_ref, slot_ref):
    qb = pl.program_id(2)
    q0 = qb * Q_BLOCK
    rows = HEADS_PER_GROUP * Q_BLOCK
    n_cmp = kc_ref.shape[2]
    n_sel = expand_ref.shape[1]

    qblk = q_ref[0]
    slope_cols = slope_ref[0]
    q4 = jnp.concatenate(
        [jnp.concatenate([qblk[:, r * HEAD_DIM:(r + 1) * HEAD_DIM],
                          jnp.broadcast_to(slope_cols[r:r + 1, :], (Q_BLOCK, HEAD_DIM))], axis=1)
         for r in range(HEADS_PER_GROUP)], axis=0)
    t_row = q0 + (lax.broadcasted_iota(jnp.int32, (rows, LANES), 0) & (Q_BLOCK - 1))
    tq = q0 + lax.broadcasted_iota(jnp.int32, (Q_BLOCK, 1), 0)

    cmp_end = lax.broadcasted_iota(jnp.int32, (1, n_cmp), 1) * CMP_STRIDE + (CMP_BLOCK - 1)
    pc4, ov = _dense_branch(q4, kc_ref[0, 0], vc_ref[0, 0],
                            jnp.where(tq >= cmp_end, 0.0, NEG_BIG))
    o_cmp = jnp.where(t_row >= CMP_BLOCK - 1, ov[:, :LANES] / ov[:, LANES:], 0.0)

    imp_l = lax.dot_general(mcs_t_ref[...], pc4, _NT, preferred_element_type=F32)
    imp = None
    for r in range(HEADS_PER_GROUP):
        cs = slice(r * Q_BLOCK, (r + 1) * Q_BLOCK)
        part = imp_l[:n_sel, cs] / imp_l[n_sel:n_sel + 1, cs]
        imp = part if imp is None else imp + part
    blk = lax.broadcasted_iota(jnp.int32, (n_sel, Q_BLOCK), 0)
    tq_l = q0 + lax.broadcasted_iota(jnp.int32, (n_sel, Q_BLOCK), 1)
    cur = tq_l >> 6
    forced = (blk == 0) | (blk == cur) | (blk == cur - 1)
    valid = blk * SEL_BLOCK <= tq_l
    score = jnp.where(forced, FORCED_SCORE, jnp.where(valid, imp, -1.0))
    n_slabs = n_sel // SUBLANES
    slabs = [score[s * SUBLANES:(s + 1) * SUBLANES, :] for s in range(n_slabs)]
    ranks = [jnp.zeros((SUBLANES, Q_BLOCK), F32) for _ in range(n_slabs)]
    sub = lax.broadcasted_iota(jnp.int32, (SUBLANES, Q_BLOCK), 0)
    for j in range(n_sel):
        other = jnp.broadcast_to(score[j:j + 1, :], (SUBLANES, Q_BLOCK))
        for s in range(n_slabs):
            if s * SUBLANES > j:
                ahead = jnp.where(other >= slabs[s], 1.0, 0.0)
            elif (s + 1) * SUBLANES - 1 <= j:
                ahead = jnp.where(other > slabs[s], 1.0, 0.0)
            else:
                ahead = jnp.where(sub > j - s * SUBLANES, jnp.where(other >= slabs[s], 1.0, 0.0),
                                  jnp.where(other > slabs[s], 1.0, 0.0))
            ranks[s] = ranks[s] + ahead
    rank = jnp.concatenate(ranks, axis=0)
    chosen_t = jnp.where(valid, jnp.where(rank < float(min(SEL_TOPN, n_sel)), 1.0, 0.0), 0.0)
    chosen_ref[...] = chosen_t.T.astype(BF16)
    q4_ref[...] = q4

    n_chunks = expand_ref.shape[0]
    per_chunk = jnp.dot(chunk_of_ref[...], chosen_t.astype(BF16), preferred_element_type=F32)
    live = jnp.max(per_chunk, axis=1, keepdims=True) > 0.0
    weight = (1 << lax.broadcasted_iota(jnp.int32, (n_chunks, 1), 0)).astype(F32)
    live_bits = jnp.sum(jnp.where(live, weight, 0.0)).astype(jnp.int32)

    n_live = 0
    for c in range(n_chunks):
        slot_ref[n_live] = c
        n_live = n_live + ((live_bits >> c) & 1)
    slot_ref[n_live] = 0
    n_pairs = (n_live + 1) // 2

    w0 = pl.multiple_of(jnp.maximum(q0 - WINDOW, 0), Q_BLOCK)
    kwin = kw_ref[0, 0, pl.ds(w0, WIN_KEYS), :]
    vwin = vw_ref[0, 0, pl.ds(w0, WIN_KEYS), :]
    d = tq - (w0 + lax.broadcasted_iota(jnp.int32, (1, WIN_KEYS), 1))
    bias_w = jnp.where(d >= 0, jnp.where(d < WINDOW, 0.0, NEG_BIG), NEG_BIG)
    _, wv = _dense_branch(q4, kwin, vwin, bias_w)
    o_win = wv[:, :LANES] / wv[:, LANES:]

    for r in range(HEADS_PER_GROUP):
        for j in range(3):
            gate_b_ref[3 * r + j] = jnp.broadcast_to(gate_ref[0, :, 3 * r + j:3 * r + j + 1],
                                                     (Q_BLOCK, LANES))

    m_ref[...] = jnp.full(m_ref.shape, NEG_BIG, F32)

    def score_body(i, carry):
        lane_max = [None] * HEADS_PER_GROUP
        for half in range(2):
            slot = 2 * i + half
            c = slot_ref[slot]
            real = slot < n_live
            k0 = pl.multiple_of(c * SEL_CHUNK, SEL_CHUNK)
            k = ks_ref[0, 0, pl.ds(k0, SEL_CHUNK), :]
            s4c = lax.dot_general(q4_ref[...], k, _NT, preferred_element_type=F32)
            hit = jnp.dot(chosen_ref[...], expand_ref[c], preferred_element_type=F32)
            pos = k0 + lax.broadcasted_iota(jnp.int32, (1, SEL_CHUNK), 1)
            bias = jnp.where(pos <= tq, (hit - 1.0) * (-NEG_BIG), NEG_BIG)
            bias = jnp.where(real, bias, NEG_BIG)
            for r in range(HEADS_PER_GROUP):
                rs = slice(r * Q_BLOCK, (r + 1) * Q_BLOCK)
                s = s4c[rs] + bias
                s_ref[slot, rs, :] = s
                mx = jnp.maximum(s[:, :LANES], s[:, LANES:])
                lane_max[r] = mx if lane_max[r] is None else jnp.maximum(lane_max[r], mx)
        for r in range(HEADS_PER_GROUP):
            rs = slice(r * Q_BLOCK, (r + 1) * Q_BLOCK)
            m_ref[rs, :] = jnp.maximum(m_ref[rs, :], lane_max[r])
        return carry

    lax.fori_loop(0, n_pairs, score_body, 0)
    m_ref[...] = jnp.broadcast_to(jnp.max(m_ref[...], axis=-1, keepdims=True), m_ref.shape)

    acc_ref[...] = jnp.zeros(acc_ref.shape, F32)

    def value_body(i, carry):
        pv = None
        for half in range(2):
            slot = 2 * i + half
            v0 = pl.multiple_of(slot_ref[slot] * SEL_CHUNK, SEL_CHUNK)
            v = vs_ref[0, 0, pl.ds(v0, SEL_CHUNK), :]
            p = jnp.concatenate([jnp.exp2(s_ref[slot, :, :LANES] - m_ref[...]),
                                 jnp.exp2(s_ref[slot, :, LANES:] - m_ref[...])],
                                axis=1).astype(BF16)
            d = jnp.dot(p, v, preferred_element_type=F32)
            pv = d if pv is None else pv + d
        acc_ref[...] += pv
        return carry

    lax.fori_loop(0, n_pairs, value_body, 0)
    o_sel = acc_ref[:, :LANES] / acc_ref[:, LANES:]

    lane = lax.broadcasted_iota(jnp.int32, (Q_BLOCK, LANES), 1)
    gated = []
    for r in range(HEADS_PER_GROUP):
        rs = slice(r * Q_BLOCK, (r + 1) * Q_BLOCK)
        gated.append(gate_b_ref[3 * r] * o_cmp[rs] + gate_b_ref[3 * r + 1] * o_sel[rs]
                     + gate_b_ref[3 * r + 2] * o_win[rs])
    o_ref[0] = jnp.concatenate(
        [jnp.where(lane < HEAD_DIM, gated[2 * i], gated[2 * i + 1])
         for i in range(HEADS_PER_GROUP // 2)], axis=-1)


def _attention(q, cmp_k, cmp_v, keys, values, gates):
    b, t, _ = q.shape
    n_cmp = cmp_k.shape[2]
    n_sel = t // SEL_BLOCK
    n_real_cmp = (t - CMP_BLOCK) // CMP_STRIDE + 1

    cs = np.arange(n_cmp)[None, :] * CMP_STRIDE
    js = np.arange(n_sel)[:, None] * SEL_BLOCK
    ov = np.minimum(cs + CMP_BLOCK, js + SEL_BLOCK) - np.maximum(cs, js)
    mcs_t = np.zeros((n_sel + SUBLANES, n_cmp), np.float32)
    mcs_t[:n_sel] = np.maximum(ov, 0).astype(np.float32) / CMP_BLOCK
    mcs_t[:n_sel, n_real_cmp:] = 0.0
    mcs_t[n_sel] = 1.0
    key_blk = (np.arange(t) // SEL_BLOCK).reshape(t // SEL_CHUNK, 1, SEL_CHUNK)
    expand = (key_blk == np.arange(n_sel)[None, :, None]).astype(np.float32)
    n_chunks = t // SEL_CHUNK
    chunk_of = (np.arange(n_sel)[None, :] // (SEL_CHUNK // SEL_BLOCK)
                == np.arange(n_chunks)[:, None]).astype(np.float32)
    rows = HEADS_PER_GROUP * Q_BLOCK

    slope_l2 = LOG2E * np.power(2.0, -8.0 * np.arange(1, N_Q_HEADS + 1) / N_Q_HEADS)
    hi = slope_l2.astype(BF16).astype(np.float64)
    lo = (slope_l2 - hi).astype(BF16).astype(np.float64)
    cols = np.zeros((N_KV_GROUPS, SUBLANES, HEAD_DIM), np.float32)
    heads = cols[:, :HEADS_PER_GROUP].reshape(N_Q_HEADS, HEAD_DIM)
    heads[:, 0], heads[:, 1], heads[:, 2], heads[:, 3] = SEL_BLOCK * hi, SEL_BLOCK * lo, hi, lo
    cols[:, :HEADS_PER_GROUP] = heads.reshape(N_KV_GROUPS, HEADS_PER_GROUP, HEAD_DIM)

    def kv_spec(branch, rows, width):
        return pl.BlockSpec((1, 1, rows, width),
                            lambda bi, gi, qi, branch=branch: (bi, branch * N_KV_GROUPS + gi, 0, 0))

    return pl.pallas_call(
        _attn_kernel,
        grid=(b, N_KV_GROUPS, t // Q_BLOCK),
        in_specs=[
            pl.BlockSpec((1, Q_BLOCK, GROUP_WIDTH), lambda bi, gi, qi: (bi, qi, gi)),
            pl.BlockSpec((1, SUBLANES, HEAD_DIM), lambda bi, gi, qi: (gi, 0, 0)),
            kv_spec(0, n_cmp, AUG_DIM), kv_spec(0, n_cmp, WIDE_DIM),
            kv_spec(0, t, AUG_DIM), kv_spec(0, t, WIDE_DIM),
            kv_spec(1, t, AUG_DIM), kv_spec(1, t, WIDE_DIM),
            pl.BlockSpec((1, Q_BLOCK, LANES), lambda bi, gi, qi: (bi, qi, gi)),
            _const_spec(mcs_t.shape), _const_spec(expand.shape), _const_spec(chunk_of.shape),
        ],
        out_specs=pl.BlockSpec((1, Q_BLOCK, GROUP_WIDTH), lambda bi, gi, qi: (bi, qi, gi)),
        out_shape=jax.ShapeDtypeStruct((b, t, ATTN_WIDTH), F32),
        scratch_shapes=[
            pltpu.VMEM((rows, AUG_DIM), BF16),
            pltpu.VMEM((Q_BLOCK, n_sel), BF16),
            pltpu.VMEM((n_chunks, rows, SEL_CHUNK), F32),
            pltpu.VMEM((rows, LANES), F32),
            pltpu.VMEM((rows, WIDE_DIM), F32),
            pltpu.VMEM((3 * HEADS_PER_GROUP, Q_BLOCK, LANES), F32),
            pltpu.SMEM((n_chunks + 1,), jnp.int32),
        ],
        compiler_params=_params("parallel", "parallel", "arbitrary"),
        name="nsa_attn",
    )(q, jnp.asarray(cols, BF16), cmp_k, cmp_v, keys, values, keys, values, gates,
      jnp.asarray(mcs_t, BF16), jnp.asarray(expand, BF16), jnp.asarray(chunk_of, BF16))


def _lru_kernel(xr_ref, xg_ref, cw_ref, cb_ref, wa_ref, ba_ref, wx_ref, bx_ref, lam_ref,
                g_ref, o_ref, xs_ref, a_ref, b_ref, h_ref):
    ti = pl.program_id(1)
    tt = xr_ref.shape[1]

    @pl.when(ti == 0)
    def _():
        xs_ref[...] = jnp.zeros_like(xs_ref)
        h_ref[...] = jnp.zeros_like(h_ref)

    n_groups = tt // SUBLANES
    as_groups = lambda v: v.reshape(n_groups, SUBLANES, LRU_WIDTH)
    sub = lax.broadcasted_iota(jnp.int32, (n_groups, SUBLANES, LRU_WIDTH), 1)

    x = as_groups(xr_ref[0])
    last = xs_ref[...]
    xc = cb_ref[...] + cw_ref[CONV_WIDTH - 1:CONV_WIDTH, :] * x
    for j in range(CONV_WIDTH - 1):
        lag = CONV_WIDTH - 1 - j
        rot = pltpu.roll(x, lag, axis=1)
        rot_before = jnp.concatenate([pltpu.roll(last, lag, axis=0)[None], rot[:-1]], axis=0)
        xc = xc + cw_ref[j:j + 1, :] * jnp.where(sub < lag, rot_before, rot)
    xs_ref[...] = x[n_groups - 1]
    xc = xc.reshape(tt, LRU_WIDTH)

    xb = xc.astype(BF16)
    r = jax.nn.sigmoid(jnp.dot(xb, wa_ref[...], preferred_element_type=F32) + ba_ref[...])
    i = jax.nn.sigmoid(jnp.dot(xb, wx_ref[...], preferred_element_type=F32) + bx_ref[...])
    neg_lam = -lam_ref[...]
    softplus = jnp.maximum(neg_lam, 0.0) + jnp.log1p(jnp.exp(-jnp.abs(neg_lam)))
    log_a = -LRU_C * r * softplus
    a = jnp.exp(log_a)
    th = jnp.tanh(log_a)
    mult = jnp.sqrt(jnp.maximum(-2.0 * th / (1.0 - th), 0.0))
    b = mult * i * xc

    a, b = as_groups(a), as_groups(b)
    for s in (1, 2, 4):
        ok = sub >= s
        a_prev = jnp.where(ok, pltpu.roll(a, s, axis=1), 1.0)
        b_prev = jnp.where(ok, pltpu.roll(b, s, axis=1), 0.0)
        b = a * b_prev + b
        a = a * a_prev
    a_ref[...] = a.reshape(tt, LRU_WIDTH)
    b_ref[...] = b.reshape(tt, LRU_WIDTH)

    def group_body(k, h):
        r0 = pl.multiple_of(k * SUBLANES, SUBLANES)
        h8 = a_ref[pl.ds(r0, SUBLANES), :] * h + b_ref[pl.ds(r0, SUBLANES), :]
        b_ref[pl.ds(r0, SUBLANES), :] = h8
        return jnp.broadcast_to(h8[SUBLANES - 1:SUBLANES, :], (SUBLANES, LRU_WIDTH))

    h_ref[...] = lax.fori_loop(0, tt // SUBLANES, group_body, h_ref[...])
    out = b_ref[...] * jax.nn.gelu(xg_ref[0])
    o_ref[0] = _rms(out, g_ref[...]).astype(BF16)


def _lru(xr, xg, conv_w, conv_b, wa, ba, wx, bx, lam, g):
    b, t, c = xr.shape
    tt = LRU_TOKENS
    tok = pl.BlockSpec((1, tt, c), lambda bi, ti: (bi, ti, 0))
    vec = _const_spec((1, c))
    return pl.pallas_call(
        _lru_kernel,
        grid=(b, t // tt),
        in_specs=[tok, tok, _const_spec((CONV_WIDTH, c)), vec, _const_spec((c, c)), vec,
                  _const_spec((c, c)), vec, vec, vec],
        out_specs=tok,
        out_shape=jax.ShapeDtypeStruct((b, t, c), BF16),
        scratch_shapes=[pltpu.VMEM((SUBLANES, c), F32), pltpu.VMEM((tt, c), F32),
                        pltpu.VMEM((tt, c), F32), pltpu.VMEM((SUBLANES, c), F32)],
        compiler_params=_params("parallel", "arbitrary"),
        name="rg_lru",
    )(xr, xg, conv_w, conv_b, wa, ba, wx, bx, lam, g)


def _outproj_kernel(h_ref, attn_ref, lru_ref, ga_ref, post_g_ref, wo_a_ref, wo_l_ref, o_ref):
    ya = _rms(attn_ref[...], ga_ref[...]).astype(BF16)
    m = (jnp.dot(ya, wo_a_ref[...], preferred_element_type=F32)
         + jnp.dot(lru_ref[...], wo_l_ref[...], preferred_element_type=F32))
    o_ref[...] = h_ref[...] + _rms(m, post_g_ref[...])


def _outproj(h, attn, lru, attn_g, post_g, wo_a, wo_l):
    n = h.shape[0]
    tm = PROJ_TOKENS

    def tok(width):
        return pl.BlockSpec((tm, width), lambda i: (i, 0))

    return pl.pallas_call(
        _outproj_kernel,
        grid=(n // tm,),
        in_specs=[tok(D_MODEL), tok(ATTN_WIDTH), tok(LRU_WIDTH), _const_spec((1, ATTN_WIDTH)),
                  _const_spec((1, D_MODEL)), _const_spec((ATTN_WIDTH, D_MODEL)),
                  _const_spec((LRU_WIDTH, D_MODEL))],
        out_specs=tok(D_MODEL),
        out_shape=jax.ShapeDtypeStruct((n, D_MODEL), F32),
        compiler_params=_params("parallel"),
        name="outproj",
    )(h, attn, lru, attn_g, post_g, wo_a, wo_l)


def _pack_w_in(w_in):
    kv_cols = N_KV_GROUPS * HEAD_DIM
    gate_lo = ATTN_WIDTH + 6 * kv_cols
    gate_hi = gate_lo + 3 * N_Q_HEADS
    per_group = 3 * HEADS_PER_GROUP
    pad = jnp.zeros((w_in.shape[0], LANES - per_group), w_in.dtype)
    gate_slabs = []
    for gi in range(N_KV_GROUPS):
        gate_slabs += [w_in[:, gate_lo + gi * per_group:gate_lo + (gi + 1) * per_group], pad]
    return jnp.concatenate([w_in[:, :gate_lo]] + gate_slabs + [w_in[:, gate_hi:]], axis=1).astype(BF16)


def _block_diag(w):
    nb, d, e = w.shape
    eye = jnp.eye(nb, dtype=w.dtype)
    return jnp.einsum("nde,nm->ndme", w, eye).reshape(nb * d, nb * e).astype(BF16)


def _layer(h, p):
    b, t, d = h.shape
    n = b * t
    row = lambda v: v.reshape(1, -1)

    h1 = _ffn(h.reshape(n, d), row(p["ffn1_pre_g"]), row(p["ffn1_post_g"]),
              p["ffn1_w_gate"].astype(BF16), p["ffn1_w_up"].astype(BF16),
              p["ffn1_w_down"].astype(BF16))

    q, cmp_in, keys, values, gates, xr, xg = _proj(h1.reshape(b, t, d), row(p["mix_pre_g"]),
                                                   _pack_w_in(p["w_in"]))
    cmp_k, cmp_v = _compress(cmp_in, p["cmp_k_pe"], p["cmp_k_w1"].astype(BF16),
                             p["cmp_k_w2"].astype(BF16), p["cmp_v_pe"],
                             p["cmp_v_w1"].astype(BF16), p["cmp_v_w2"].astype(BF16))
    attn = _attention(q, cmp_k, cmp_v, keys, values, gates)
    lru = _lru(xr, xg, p["conv_w"], row(p["conv_b"]), _block_diag(p["lru_w_a"]),
               row(p["lru_b_a"]), _block_diag(p["lru_w_x"]), row(p["lru_b_x"]),
               row(p["lru_lambda"]), row(p["lru_out_g"]))
    w_out = p["w_out"].astype(BF16)
    h2 = _outproj(h1, attn.reshape(n, ATTN_WIDTH), lru.reshape(n, LRU_WIDTH),
                  row(p["attn_out_g"]), row(p["mix_post_g"]),
                  w_out[:ATTN_WIDTH], w_out[ATTN_WIDTH:])

    h3 = _ffn(h2, row(p["ffn2_pre_g"]), row(p["ffn2_post_g"]),
              p["ffn2_w_gate"].astype(BF16), p["ffn2_w_up"].astype(BF16),
              p["ffn2_w_down"].astype(BF16))
    return h3.reshape(b, t, d)


_PARAM_NAMES = (
    "ffn1_pre_g", "ffn1_post_g", "ffn1_w_gate", "ffn1_w_up", "ffn1_w_down",
    "mix_pre_g", "mix_post_g", "w_in", "cmp_k_pe", "cmp_k_w1", "cmp_k_w2",
    "cmp_v_pe", "cmp_v_w1", "cmp_v_w2", "conv_w", "conv_b", "lru_w_a", "lru_b_a",
    "lru_w_x", "lru_b_x", "lru_lambda", "attn_out_g", "lru_out_g", "w_out",
    "ffn2_pre_g", "ffn2_post_g", "ffn2_w_gate", "ffn2_w_up", "ffn2_w_down",
)


def kernel(x, ffn1_pre_g, ffn1_post_g, ffn1_w_gate, ffn1_w_up, ffn1_w_down, mix_pre_g, mix_post_g, w_in, cmp_k_pe, cmp_k_w1, cmp_k_w2, cmp_v_pe, cmp_v_w1, cmp_v_w2, conv_w, conv_b, lru_w_a, lru_b_a, lru_w_x, lru_b_x, lru_lambda, attn_out_g, lru_out_g, w_out, ffn2_pre_g, ffn2_post_g, ffn2_w_gate, ffn2_w_up, ffn2_w_down):
    stacked = dict(zip(_PARAM_NAMES, (
        ffn1_pre_g, ffn1_post_g, ffn1_w_gate, ffn1_w_up, ffn1_w_down, mix_pre_g, mix_post_g,
        w_in, cmp_k_pe, cmp_k_w1, cmp_k_w2, cmp_v_pe, cmp_v_w1, cmp_v_w2, conv_w, conv_b,
        lru_w_a, lru_b_a, lru_w_x, lru_b_x, lru_lambda, attn_out_g, lru_out_g, w_out,
        ffn2_pre_g, ffn2_post_g, ffn2_w_gate, ffn2_w_up, ffn2_w_down)))
    h = x
    for layer in range(ffn1_pre_g.shape[0]):
        h = _layer(h, {k: v[layer] for k, v in stacked.items()})
    return h
```

```python
import functools

import numpy as np
import jax
import jax.numpy as jnp
from jax import lax
from jax.experimental import pallas as pl
from jax.experimental.pallas import tpu as pltpu

F32 = jnp.float32
BF16 = jnp.bfloat16

D_MODEL = 1024
N_Q_HEADS = 8
HEAD_DIM = 64
N_KV_GROUPS = 2
HEADS_PER_GROUP = N_Q_HEADS // N_KV_GROUPS
ATTN_WIDTH = N_Q_HEADS * HEAD_DIM
GROUP_WIDTH = HEADS_PER_GROUP * HEAD_DIM
CMP_BLOCK = 32
CMP_STRIDE = 16
CMP_HIDDEN = 256
SEL_BLOCK = 64
SEL_TOPN = 16
WINDOW = 512
Q_BLOCK = 256
LRU_WIDTH = 512
LRU_BLOCKS = 8
CONV_WIDTH = 4
LRU_C = 8.0
D_FF = 2816
NORM_EPS = 1e-6

LANES = 128
SUBLANES = 8
VMEM_LIMIT_BYTES = 56 * 1024 * 1024

NEG_BIG = -1e30
FORCED_SCORE = 3e38
LOG2E = 1.4426950408889634
AUG_DIM = 2 * HEAD_DIM
WIDE_DIM = 4 * HEAD_DIM

FFN_TOKENS = 512
FFN_CHUNK = 256
LRU_TOKENS = 512
SEL_CHUNK = 256
WIN_KEYS = WINDOW + Q_BLOCK

COL_Q = 0
COL_CMP = COL_Q + ATTN_WIDTH
COL_KV = COL_CMP + 2 * N_KV_GROUPS * HEAD_DIM
COL_GATE = COL_KV + 4 * N_KV_GROUPS * HEAD_DIM
COL_XR = COL_GATE + N_KV_GROUPS * LANES
COL_XG = COL_XR + LRU_WIDTH
PROJ_WIDTH = COL_XG + LRU_WIDTH


def _rms(x, g):
    ms = jnp.mean(x * x, axis=-1, keepdims=True)
    return x * lax.rsqrt(ms + NORM_EPS) * g


def _const_spec(shape):
    nd = len(shape)
    return pl.BlockSpec(shape, lambda *_: (0,) * nd, pipeline_mode=pl.Buffered(1))


def _params(*sem):
    return pltpu.CompilerParams(dimension_semantics=sem, vmem_limit_bytes=VMEM_LIMIT_BYTES)


def _ffn_half_step(x, pre_g_ref, post_g_ref, wg_ref, wu_ref, wd_ref, acc_ref):
    xb = _rms(x, pre_g_ref[...]).astype(BF16)
    for c in range(D_FF // FFN_CHUNK):
        sl = slice(c * FFN_CHUNK, (c + 1) * FFN_CHUNK)
        gate = jnp.dot(xb, wg_ref[:, sl], preferred_element_type=F32)
        up = jnp.dot(xb, wu_ref[:, sl], preferred_element_type=F32)
        act = (jax.nn.silu(gate) * up).astype(BF16)
        contrib = jnp.dot(act, wd_ref[sl, :], preferred_element_type=F32)
        if c == 0:
            acc_ref[...] = contrib
        else:
            acc_ref[...] += contrib
    return x + 0.5 * _rms(acc_ref[...], post_g_ref[...])


def _ffn_weight_specs():
    return [_const_spec((1, D_MODEL)), _const_spec((1, D_MODEL)), _const_spec((D_MODEL, D_FF)),
            _const_spec((D_MODEL, D_FF)), _const_spec((D_FF, D_MODEL))]


def _key_tail(pos, rows):
    lane = lax.broadcasted_iota(jnp.int32, (rows, HEAD_DIM), 1)
    hi = (pos >> 6).astype(F32)
    lo = (pos & (SEL_BLOCK - 1)).astype(F32)
    return jnp.where(lane < 2, hi, jnp.where(lane < 4, lo, 0.0))


def _wide_value(v):
    return jnp.concatenate([v, v, jnp.ones((v.shape[0], LANES), F32)], axis=1).astype(BF16)


def _ffn_proj_kernel(x_ref, pre_g_ref, post_g_ref, wg_ref, wu_ref, wd_ref, g_ref, w_ref,
                     h_ref, q_ref, cmp_ref, k_ref, v_ref, gate_ref, xr_ref, xg_ref, acc_ref):
    tm = x_ref.shape[1]
    h = _ffn_half_step(x_ref[0], pre_g_ref, post_g_ref, wg_ref, wu_ref, wd_ref, acc_ref)
    h_ref[0] = h
    hb = _rms(h, g_ref[...]).astype(BF16)
    p = jnp.dot(hb, w_ref[...], preferred_element_type=F32)
    q_ref[0] = (p[:, COL_Q:COL_CMP] * (HEAD_DIM ** -0.5 * LOG2E)).astype(BF16)
    cmp_ref[0] = p[:, COL_CMP:COL_KV]
    pos = pl.program_id(1) * tm + lax.broadcasted_iota(jnp.int32, (tm, 1), 0)
    key_tail = _key_tail(pos, tm)
    for i in range(4 * N_KV_GROUPS):
        lo = COL_KV + i * HEAD_DIM
        x = p[:, lo:lo + HEAD_DIM]
        branch, is_value, gi = i // (2 * N_KV_GROUPS), (i // N_KV_GROUPS) % 2, i % N_KV_GROUPS
        if is_value:
            v_ref[0, branch * N_KV_GROUPS + gi] = _wide_value(x)
        else:
            k_ref[0, branch * N_KV_GROUPS + gi] = jnp.concatenate([x, key_tail], axis=1).astype(BF16)
    gate_ref[0] = jax.nn.sigmoid(p[:, COL_GATE:COL_XR])
    xr_ref[0] = p[:, COL_XR:COL_XG]
    xg_ref[0] = p[:, COL_XG:PROJ_WIDTH]


def _ffn_proj(x, ffn_weights, g, w_packed):
    b, t, _ = x.shape
    tm = FFN_TOKENS

    def tok(width):
        return pl.BlockSpec((1, tm, width), lambda bi, ti: (bi, ti, 0))

    return pl.pallas_call(
        _ffn_proj_kernel,
        grid=(b, t // tm),
        in_specs=[tok(D_MODEL)] + _ffn_weight_specs()
        + [_const_spec((1, D_MODEL)), _const_spec((D_MODEL, PROJ_WIDTH))],
        out_specs=[
            tok(D_MODEL),
            tok(ATTN_WIDTH),
            tok(2 * N_KV_GROUPS * HEAD_DIM),
            pl.BlockSpec((1, 2 * N_KV_GROUPS, tm, AUG_DIM), lambda bi, ti: (bi, 0, ti, 0)),
            pl.BlockSpec((1, 2 * N_KV_GROUPS, tm, WIDE_DIM), lambda bi, ti: (bi, 0, ti, 0)),
            tok(N_KV_GROUPS * LANES),
            tok(LRU_WIDTH),
            tok(LRU_WIDTH),
        ],
        out_shape=[
            jax.ShapeDtypeStruct((b, t, D_MODEL), F32),
            jax.ShapeDtypeStruct((b, t, ATTN_WIDTH), BF16),
            jax.ShapeDtypeStruct((b, t, 2 * N_KV_GROUPS * HEAD_DIM), F32),
            jax.ShapeDtypeStruct((b, 2 * N_KV_GROUPS, t, AUG_DIM), BF16),
            jax.ShapeDtypeStruct((b, 2 * N_KV_GROUPS, t, WIDE_DIM), BF16),
            jax.ShapeDtypeStruct((b, t, N_KV_GROUPS * LANES), F32),
            jax.ShapeDtypeStruct((b, t, LRU_WIDTH), F32),
            jax.ShapeDtypeStruct((b, t, LRU_WIDTH), F32),
        ],
        scratch_shapes=[pltpu.VMEM((tm, D_MODEL), F32)],
        compiler_params=_params("parallel", "parallel"),
        name="ffn_proj",
    )(x, *ffn_weights, g, w_packed)


def _compress_kernel(xk_ref, xv_ref, kpe_ref, kw1_ref, kw2_ref, vpe_ref, vw1_ref, vw2_ref,
                     ok_ref, ov_ref):
    n_chunks = xk_ref.shape[1] // CMP_STRIDE
    half = CMP_BLOCK // 2
    kinds = ((xk_ref, kpe_ref, kw1_ref, kw2_ref), (xv_ref, vpe_ref, vw1_ref, vw2_ref))
    top = [jnp.zeros((n_chunks, CMP_HIDDEN), F32) for _ in range(4)]
    bot = [jnp.zeros((n_chunks, CMP_HIDDEN), F32) for _ in range(4)]
    for l in range(half):
        rows = [ref[0, pl.ds(l, n_chunks, stride=CMP_STRIDE), :] for ref in (xk_ref, xv_ref)]
        for s in range(4):
            _, pe_ref, w1_ref, _ = kinds[s // N_KV_GROUPS]
            gi = s % N_KV_GROUPS
            xs = rows[s // N_KV_GROUPS][:, gi * HEAD_DIM:(gi + 1) * HEAD_DIM]
            x_top = (xs + pe_ref[l:l + 1, :]).astype(BF16)
            x_bot = (xs + pe_ref[half + l:half + l + 1, :]).astype(BF16)
            top[s] += jnp.dot(x_top, w1_ref[l * HEAD_DIM:(l + 1) * HEAD_DIM, :],
                              preferred_element_type=F32)
            bot[s] += jnp.dot(x_bot, w1_ref[(half + l) * HEAD_DIM:(half + l + 1) * HEAD_DIM, :],
                              preferred_element_type=F32)
    row = lax.broadcasted_iota(jnp.int32, (n_chunks, HEAD_DIM), 0)
    cmp_end = lax.broadcasted_iota(jnp.int32, (n_chunks, 1), 0) * CMP_STRIDE + (CMP_BLOCK - 1)
    key_tail = _key_tail(cmp_end, n_chunks)
    for s in range(4):
        w2_ref = kinds[s // N_KV_GROUPS][3]
        hidden = top[s] + pltpu.roll(bot[s], n_chunks - 1, axis=0)
        out = jnp.dot(jax.nn.gelu(hidden).astype(BF16), w2_ref[...], preferred_element_type=F32)
        out = jnp.where(row < n_chunks - 1, out, 0.0)
        if s // N_KV_GROUPS == 0:
            ok_ref[0, s % N_KV_GROUPS] = jnp.concatenate([out, key_tail], axis=1).astype(BF16)
        else:
            ov_ref[0, s % N_KV_GROUPS] = _wide_value(out)


def _compress(cmp_in, k_pe, k_w1, k_w2, v_pe, v_w1, v_w2):
    b, t, _ = cmp_in.shape
    n_chunks = t // CMP_STRIDE
    kv_cols = N_KV_GROUPS * HEAD_DIM
    return pl.pallas_call(
        _compress_kernel,
        grid=(b,),
        in_specs=[pl.BlockSpec((1, t, kv_cols), lambda bi: (bi, 0, 0)),
                  pl.BlockSpec((1, t, kv_cols), lambda bi: (bi, 0, 1)),
                  _const_spec(k_pe.shape), _const_spec(k_w1.shape), _const_spec(k_w2.shape),
                  _const_spec(v_pe.shape), _const_spec(v_w1.shape), _const_spec(v_w2.shape)],
        out_specs=[pl.BlockSpec((1, N_KV_GROUPS, n_chunks, AUG_DIM), lambda bi: (bi, 0, 0, 0)),
                   pl.BlockSpec((1, N_KV_GROUPS, n_chunks, WIDE_DIM), lambda bi: (bi, 0, 0, 0))],
        out_shape=[jax.ShapeDtypeStruct((b, N_KV_GROUPS, n_chunks, AUG_DIM), BF16),
                   jax.ShapeDtypeStruct((b, N_KV_GROUPS, n_chunks, WIDE_DIM), BF16)],
        compiler_params=_params("parallel"),
        name="compress",
    )(cmp_in, cmp_in, k_pe, k_w1, k_w2, v_pe, v_w1, v_w2)


_NT = (((1,), (1,)), ((), ()))


def _softmax_numerators(s, bias):
    n_slabs = s.shape[1] // LANES
    probs = []
    for r in range(s.shape[0] // Q_BLOCK):
        sb = s[r * Q_BLOCK:(r + 1) * Q_BLOCK] + bias
        slabs = [sb[:, j * LANES:(j + 1) * LANES] for j in range(n_slabs)]
        lane_max = functools.reduce(jnp.maximum, slabs)
        m = jnp.broadcast_to(jnp.max(lane_max, axis=-1, keepdims=True), (Q_BLOCK, LANES))
        probs.append(jnp.concatenate([jnp.exp2(x - m) for x in slabs], axis=1).astype(BF16))
    return jnp.concatenate(probs, axis=0)


def _pair_scores(q4, k):
    half = q4.shape[0] // 2
    return [lax.dot_general(q4[i * half:(i + 1) * half], k, _NT, preferred_element_type=F32)
            for i in range(2)]


def _pair_attend(scores, v, bias):
    probs = [_softmax_numerators(s, bias) for s in scores]
    outs = [jnp.dot(p, v, preferred_element_type=F32) for p in probs]
    return jnp.concatenate(probs, axis=0), jnp.concatenate(outs, axis=0)


def _attn_kernel(q_ref, slope_ref, kc_ref, vc_ref, ks_ref, vs_ref, kw_ref, vw_ref, gate_ref,
                 mcs_t_ref, expand_ref, chunk_of_ref, o_ref,
                 q4_ref, chosen_ref, s_ref, m_ref, acc_ref, gate_b_ref, slot_ref):
    qb = pl.program_id(2)
    q0 = qb * Q_BLOCK
    rows = HEADS_PER_GROUP * Q_BLOCK
    n_cmp = kc_ref.shape[2]
    n_sel = expand_ref.shape[1]

    qblk = q_ref[0]
    slope_cols = slope_ref[0]
    q4 = jnp.concatenate(
        [jnp.concatenate([qblk[:, r * HEAD_DIM:(r + 1) * HEAD_DIM],
                          jnp.broadcast_to(slope_cols[r:r + 1, :], (Q_BLOCK, HEAD_DIM))], axis=1)
         for r in range(HEADS_PER_GROUP)], axis=0)
    t_row = q0 + (lax.broadcasted_iota(jnp.int32, (rows, LANES), 0) & (Q_BLOCK - 1))
    tq = q0 + lax.broadcasted_iota(jnp.int32, (Q_BLOCK, 1), 0)

    w0 = pl.multiple_of(jnp.maximum(q0 - WINDOW, 0), Q_BLOCK)
    win_scores = _pair_scores(q4, kw_ref[0, 0, pl.ds(w0, WIN_KEYS), :])

    cmp_end = lax.broadcasted_iota(jnp.int32, (1, n_cmp), 1) * CMP_STRIDE + (CMP_BLOCK - 1)
    pc4, ov = _pair_attend(_pair_scores(q4, kc_ref[0, 0]), vc_ref[0, 0],
                           jnp.where(tq >= cmp_end, 0.0, NEG_BIG))
    o_cmp = jnp.where(t_row >= CMP_BLOCK - 1, ov[:, :LANES] / ov[:, LANES:], 0.0)

    imp_l = lax.dot_general(mcs_t_ref[...], pc4, _NT, preferred_element_type=F32)
    imp = None
    for r in range(HEADS_PER_GROUP):
        cs = slice(r * Q_BLOCK, (r + 1) * Q_BLOCK)
        part = imp_l[:n_sel, cs] / imp_l[n_sel:n_sel + 1, cs]
        imp = part if imp is None else imp + part
    blk = lax.broadcasted_iota(jnp.int32, (n_sel, Q_BLOCK), 0)
    tq_l = q0 + lax.broadcasted_iota(jnp.int32, (n_sel, Q_BLOCK), 1)
    cur = tq_l >> 6
    forced = (blk == 0) | (blk == cur) | (blk == cur - 1)
    valid = blk * SEL_BLOCK <= tq_l
    score = jnp.where(forced, FORCED_SCORE, jnp.where(valid, imp, -1.0))
    n_slabs = n_sel // SUBLANES
    slabs = [score[s * SUBLANES:(s + 1) * SUBLANES, :] for s in range(n_slabs)]
    ranks = [jnp.zeros((SUBLANES, Q_BLOCK), F32) for _ in range(n_slabs)]
    sub = lax.broadcasted_iota(jnp.int32, (SUBLANES, Q_BLOCK), 0)
    for j in range(n_sel):
        other = jnp.broadcast_to(score[j:j + 1, :], (SUBLANES, Q_BLOCK))
        for s in range(n_slabs):
            if s * SUBLANES > j:
                ahead = jnp.where(other >= slabs[s], 1.0, 0.0)
            elif (s + 1) * SUBLANES - 1 <= j:
                ahead = jnp.where(other > slabs[s], 1.0, 0.0)
            else:
                ahead = jnp.where(sub > j - s * SUBLANES, jnp.where(other >= slabs[s], 1.0, 0.0),
                                  jnp.where(other > slabs[s], 1.0, 0.0))
            ranks[s] = ranks[s] + ahead
    rank = jnp.concatenate(ranks, axis=0)
    chosen_t = jnp.where(valid, jnp.where(rank < float(min(SEL_TOPN, n_sel)), 1.0, 0.0), 0.0)
    chosen_ref[...] = chosen_t.T.astype(BF16)
    q4_ref[...] = q4

    n_chunks = expand_ref.shape[0]
    per_chunk = jnp.dot(chunk_of_ref[...], chosen_t.astype(BF16), preferred_element_type=F32)
    live = jnp.max(per_chunk, axis=1, keepdims=True) > 0.0
    weight = (1 << lax.broadcasted_iota(jnp.int32, (n_chunks, 1), 0)).astype(F32)
    live_bits = jnp.sum(jnp.where(live, weight, 0.0)).astype(jnp.int32)

    n_live = 0
    for c in range(n_chunks):
        slot_ref[n_live] = c
        n_live = n_live + ((live_bits >> c) & 1)
    slot_ref[n_live] = 0
    n_pairs = (n_live + 1) // 2

    vwin = vw_ref[0, 0, pl.ds(w0, WIN_KEYS), :]
    d = tq - (w0 + lax.broadcasted_iota(jnp.int32, (1, WIN_KEYS), 1))
    bias_w = jnp.where(d >= 0, jnp.where(d < WINDOW, 0.0, NEG_BIG), NEG_BIG)
    _, wv = _pair_attend(win_scores, vwin, bias_w)
    o_win = wv[:, :LANES] / wv[:, LANES:]

    for r in range(HEADS_PER_GROUP):
        for j in range(3):
            gate_b_ref[3 * r + j] = jnp.broadcast_to(gate_ref[0, :, 3 * r + j:3 * r + j + 1],
                                                     (Q_BLOCK, LANES))

    m_ref[...] = jnp.full(m_ref.shape, NEG_BIG, F32)

    def score_body(i, carry):
        lane_max = [None] * HEADS_PER_GROUP
        starts, biases = [], []
        for half in range(2):
            slot = 2 * i + half
            c = slot_ref[slot]
            k0 = pl.multiple_of(c * SEL_CHUNK, SEL_CHUNK)
            hit = jnp.dot(chosen_ref[...], expand_ref[c], preferred_element_type=F32)
            pos = k0 + lax.broadcasted_iota(jnp.int32, (1, SEL_CHUNK), 1)
            bias = jnp.where(pos <= tq, (hit - 1.0) * (-NEG_BIG), NEG_BIG)
            biases.append(jnp.where(slot < n_live, bias, NEG_BIG))
            starts.append(k0)
        for half in range(2):
            slot = 2 * i + half
            bias = biases[half]
            k = ks_ref[0, 0, pl.ds(starts[half], SEL_CHUNK), :]
            s4c = lax.dot_general(q4_ref[...], k, _NT, preferred_element_type=F32)
            for r in range(HEADS_PER_GROUP):
                rs = slice(r * Q_BLOCK, (r + 1) * Q_BLOCK)
                s = s4c[rs] + bias
                s_ref[slot, rs, :] = s
                mx = jnp.maximum(s[:, :LANES], s[:, LANES:])
                lane_max[r] = mx if lane_max[r] is None else jnp.maximum(lane_max[r], mx)
        for r in range(HEADS_PER_GROUP):
            rs = slice(r * Q_BLOCK, (r + 1) * Q_BLOCK)
            m_ref[rs, :] = jnp.maximum(m_ref[rs, :], lane_max[r])
        return carry

    lax.fori_loop(0, n_pairs, score_body, 0)
    m_ref[...] = jnp.broadcast_to(jnp.max(m_ref[...], axis=-1, keepdims=True), m_ref.shape)

    acc_ref[...] = jnp.zeros(acc_ref.shape, F32)

    def value_body(i, carry):
        pv = None
        for half in range(2):
            slot = 2 * i + half
            v0 = pl.multiple_of(slot_ref[slot] * SEL_CHUNK, SEL_CHUNK)
            v = vs_ref[0, 0, pl.ds(v0, SEL_CHUNK), :]
            p = jnp.concatenate([jnp.exp2(s_ref[slot, :, :LANES] - m_ref[...]),
                                 jnp.exp2(s_ref[slot, :, LANES:] - m_ref[...])],
                                axis=1).astype(BF16)
            d = jnp.dot(p, v, preferred_element_type=F32)
            pv = d if pv is None else pv + d
        acc_ref[...] += pv
        return carry

    lax.fori_loop(0, n_pairs, value_body, 0)
    o_sel = acc_ref[:, :LANES] / acc_ref[:, LANES:]

    lane = lax.broadcasted_iota(jnp.int32, (Q_BLOCK, LANES), 1)
    gated = []
    for r in range(HEADS_PER_GROUP):
        rs = slice(r * Q_BLOCK, (r + 1) * Q_BLOCK)
        gated.append(gate_b_ref[3 * r] * o_cmp[rs] + gate_b_ref[3 * r + 1] * o_sel[rs]
                     + gate_b_ref[3 * r + 2] * o_win[rs])
    o_ref[0] = jnp.concatenate(
        [jnp.where(lane < HEAD_DIM, gated[2 * i], gated[2 * i + 1])
         for i in range(HEADS_PER_GROUP // 2)], axis=-1)


def _attention(q, cmp_k, cmp_v, keys, values, gates):
    b, t, _ = q.shape
    n_cmp = cmp_k.shape[2]
    n_sel = t // SEL_BLOCK
    n_real_cmp = (t - CMP_BLOCK) // CMP_STRIDE + 1

    cs = np.arange(n_cmp)[None, :] * CMP_STRIDE
    js = np.arange(n_sel)[:, None] * SEL_BLOCK
    ov = np.minimum(cs + CMP_BLOCK, js + SEL_BLOCK) - np.maximum(cs, js)
    mcs_t = np.zeros((n_sel + SUBLANES, n_cmp), np.float32)
    mcs_t[:n_sel] = np.maximum(ov, 0).astype(np.float32) / CMP_BLOCK
    mcs_t[:n_sel, n_real_cmp:] = 0.0
    mcs_t[n_sel] = 1.0
    key_blk = (np.arange(t) // SEL_BLOCK).reshape(t // SEL_CHUNK, 1, SEL_CHUNK)
    expand = (key_blk == np.arange(n_sel)[None, :, None]).astype(np.float32)
    n_chunks = t // SEL_CHUNK
    chunk_of = (np.arange(n_sel)[None, :] // (SEL_CHUNK // SEL_BLOCK)
                == np.arange(n_chunks)[:, None]).astype(np.float32)
    rows = HEADS_PER_GROUP * Q_BLOCK

    slope_l2 = LOG2E * np.power(2.0, -8.0 * np.arange(1, N_Q_HEADS + 1) / N_Q_HEADS)
    hi = slope_l2.astype(BF16).astype(np.float64)
    lo = (slope_l2 - hi).astype(BF16).astype(np.float64)
    cols = np.zeros((N_KV_GROUPS, SUBLANES, HEAD_DIM), np.float32)
    heads = cols[:, :HEADS_PER_GROUP].reshape(N_Q_HEADS, HEAD_DIM)
    heads[:, 0], heads[:, 1], heads[:, 2], heads[:, 3] = SEL_BLOCK * hi, SEL_BLOCK * lo, hi, lo
    cols[:, :HEADS_PER_GROUP] = heads.reshape(N_KV_GROUPS, HEADS_PER_GROUP, HEAD_DIM)

    def kv_spec(branch, rows, width):
        return pl.BlockSpec((1, 1, rows, width),
                            lambda bi, gi, qi, branch=branch: (bi, branch * N_KV_GROUPS + gi, 0, 0))

    return pl.pallas_call(
        _attn_kernel,
        grid=(b, N_KV_GROUPS, t // Q_BLOCK),
        in_specs=[
            pl.BlockSpec((1, Q_BLOCK, GROUP_WIDTH), lambda bi, gi, qi: (bi, qi, gi)),
            pl.BlockSpec((1, SUBLANES, HEAD_DIM), lambda bi, gi, qi: (gi, 0, 0)),
            kv_spec(0, n_cmp, AUG_DIM), kv_spec(0, n_cmp, WIDE_DIM),
            kv_spec(0, t, AUG_DIM), kv_spec(0, t, WIDE_DIM),
            kv_spec(1, t, AUG_DIM), kv_spec(1, t, WIDE_DIM),
            pl.BlockSpec((1, Q_BLOCK, LANES), lambda bi, gi, qi: (bi, qi, gi)),
            _const_spec(mcs_t.shape), _const_spec(expand.shape), _const_spec(chunk_of.shape),
        ],
        out_specs=pl.BlockSpec((1, Q_BLOCK, GROUP_WIDTH), lambda bi, gi, qi: (bi, qi, gi)),
        out_shape=jax.ShapeDtypeStruct((b, t, ATTN_WIDTH), F32),
        scratch_shapes=[
            pltpu.VMEM((rows, AUG_DIM), BF16),
            pltpu.VMEM((Q_BLOCK, n_sel), BF16),
            pltpu.VMEM((n_chunks, rows, SEL_CHUNK), F32),
            pltpu.VMEM((rows, LANES), F32),
            pltpu.VMEM((rows, WIDE_DIM), F32),
            pltpu.VMEM((3 * HEADS_PER_GROUP, Q_BLOCK, LANES), F32),
            pltpu.SMEM((n_chunks + 1,), jnp.int32),
        ],
        compiler_params=_params("parallel", "parallel", "arbitrary"),
        name="nsa_attn",
    )(q, jnp.asarray(cols, BF16), cmp_k, cmp_v, keys, values, keys, values, gates,
      jnp.asarray(mcs_t, BF16), jnp.asarray(expand, BF16), jnp.asarray(chunk_of, BF16))


def _lru_kernel(xr_ref, xg_ref, cw_ref, cb_ref, wa_ref, ba_ref, wx_ref, bx_ref, lam_ref,
                g_ref, o_ref, xs_ref, a_ref, b_ref, h_ref):
    ti = pl.program_id(1)
    tt = xr_ref.shape[1]

    @pl.when(ti == 0)
    def _():
        xs_ref[...] = jnp.zeros_like(xs_ref)
        h_ref[...] = jnp.zeros_like(h_ref)

    n_groups = tt // SUBLANES
    as_groups = lambda v: v.reshape(n_groups, SUBLANES, LRU_WIDTH)
    sub = lax.broadcasted_iota(jnp.int32, (n_groups, SUBLANES, LRU_WIDTH), 1)

    x = as_groups(xr_ref[0])
    last = xs_ref[...]
    xc = cb_ref[...] + cw_ref[CONV_WIDTH - 1:CONV_WIDTH, :] * x
    for j in range(CONV_WIDTH - 1):
        lag = CONV_WIDTH - 1 - j
        rot = pltpu.roll(x, lag, axis=1)
        rot_before = jnp.concatenate([pltpu.roll(last, lag, axis=0)[None], rot[:-1]], axis=0)
        xc = xc + cw_ref[j:j + 1, :] * jnp.where(sub < lag, rot_before, rot)
    xs_ref[...] = x[n_groups - 1]
    xc = xc.reshape(tt, LRU_WIDTH)

    xb = xc.astype(BF16)
    r = jax.nn.sigmoid(jnp.dot(xb, wa_ref[...], preferred_element_type=F32) + ba_ref[...])
    i = jax.nn.sigmoid(jnp.dot(xb, wx_ref[...], preferred_element_type=F32) + bx_ref[...])
    neg_lam = -lam_ref[...]
    softplus = jnp.maximum(neg_lam, 0.0) + jnp.log1p(jnp.exp(-jnp.abs(neg_lam)))
    log_a = -LRU_C * r * softplus
    a = jnp.exp(log_a)
    th = jnp.tanh(log_a)
    mult = jnp.sqrt(jnp.maximum(-2.0 * th / (1.0 - th), 0.0))
    b = mult * i * xc

    a, b = as_groups(a), as_groups(b)
    for s in (1, 2, 4):
        ok = sub >= s
        a_prev = jnp.where(ok, pltpu.roll(a, s, axis=1), 1.0)
        b_prev = jnp.where(ok, pltpu.roll(b, s, axis=1), 0.0)
        b = a * b_prev + b
        a = a * a_prev
    a_ref[...] = a.reshape(tt, LRU_WIDTH)
    b_ref[...] = b.reshape(tt, LRU_WIDTH)

    def group_body(k, h):
        r0 = pl.multiple_of(k * SUBLANES, SUBLANES)
        h8 = a_ref[pl.ds(r0, SUBLANES), :] * h + b_ref[pl.ds(r0, SUBLANES), :]
        b_ref[pl.ds(r0, SUBLANES), :] = h8
        return jnp.broadcast_to(h8[SUBLANES - 1:SUBLANES, :], (SUBLANES, LRU_WIDTH))

    h_ref[...] = lax.fori_loop(0, tt // SUBLANES, group_body, h_ref[...])
    out = b_ref[...] * jax.nn.gelu(xg_ref[0])
    o_ref[0] = _rms(out, g_ref[...]).astype(BF16)


def _lru(xr, xg, conv_w, conv_b, wa, ba, wx, bx, lam, g):
    b, t, c = xr.shape
    tt = LRU_TOKENS
    tok = pl.BlockSpec((1, tt, c), lambda bi, ti: (bi, ti, 0))
    vec = _const_spec((1, c))
    return pl.pallas_call(
        _lru_kernel,
        grid=(b, t // tt),
        in_specs=[tok, tok, _const_spec((CONV_WIDTH, c)), vec, _const_spec((c, c)), vec,
                  _const_spec((c, c)), vec, vec, vec],
        out_specs=tok,
        out_shape=jax.ShapeDtypeStruct((b, t, c), BF16),
        scratch_shapes=[pltpu.VMEM((SUBLANES, c), F32), pltpu.VMEM((tt, c), F32),
                        pltpu.VMEM((tt, c), F32), pltpu.VMEM((SUBLANES, c), F32)],
        compiler_params=_params("parallel", "arbitrary"),
        name="rg_lru",
    )(xr, xg, conv_w, conv_b, wa, ba, wx, bx, lam, g)


def _mix_ffn_kernel(h_ref, attn_ref, lru_ref, ga_ref, mix_g_ref, wo_a_ref, wo_l_ref,
                    pre_g_ref, post_g_ref, wg_ref, wu_ref, wd_ref, o_ref, acc_ref):
    ya = _rms(attn_ref[...], ga_ref[...]).astype(BF16)
    m = (jnp.dot(ya, wo_a_ref[...], preferred_element_type=F32)
         + jnp.dot(lru_ref[...], wo_l_ref[...], preferred_element_type=F32))
    h = h_ref[...] + _rms(m, mix_g_ref[...])
    o_ref[...] = _ffn_half_step(h, pre_g_ref, post_g_ref, wg_ref, wu_ref, wd_ref, acc_ref)


def _mix_ffn(h, attn, lru, attn_g, mix_g, wo_a, wo_l, ffn_weights):
    n = h.shape[0]
    tm = FFN_TOKENS

    def tok(width):
        return pl.BlockSpec((tm, width), lambda i: (i, 0))

    return pl.pallas_call(
        _mix_ffn_kernel,
        grid=(n // tm,),
        in_specs=[tok(D_MODEL), tok(ATTN_WIDTH), tok(LRU_WIDTH), _const_spec((1, ATTN_WIDTH)),
                  _const_spec((1, D_MODEL)), _const_spec((ATTN_WIDTH, D_MODEL)),
                  _const_spec((LRU_WIDTH, D_MODEL))] + _ffn_weight_specs(),
        out_specs=tok(D_MODEL),
        out_shape=jax.ShapeDtypeStruct((n, D_MODEL), F32),
        scratch_shapes=[pltpu.VMEM((tm, D_MODEL), F32)],
        compiler_params=_params("parallel"),
        name="mix_ffn",
    )(h, attn, lru, attn_g, mix_g, wo_a, wo_l, *ffn_weights)


def _pack_w_in(w_in):
    kv_cols = N_KV_GROUPS * HEAD_DIM
    gate_lo = ATTN_WIDTH + 6 * kv_cols
    gate_hi = gate_lo + 3 * N_Q_HEADS
    per_group = 3 * HEADS_PER_GROUP
    pad = jnp.zeros((w_in.shape[0], LANES - per_group), w_in.dtype)
    gate_slabs = []
    for gi in range(N_KV_GROUPS):
        gate_slabs += [w_in[:, gate_lo + gi * per_group:gate_lo + (gi + 1) * per_group], pad]
    return jnp.concatenate([w_in[:, :gate_lo]] + gate_slabs + [w_in[:, gate_hi:]], axis=1).astype(BF16)


def _block_diag(w):
    nb, d, e = w.shape
    eye = jnp.eye(nb, dtype=w.dtype)
    return jnp.einsum("nde,nm->ndme", w, eye).reshape(nb * d, nb * e).astype(BF16)


def _layer(h, p):
    b, t, d = h.shape
    n = b * t
    row = lambda v: v.reshape(1, -1)

    def ffn_weights(i):
        return (row(p[f"ffn{i}_pre_g"]), row(p[f"ffn{i}_post_g"]), p[f"ffn{i}_w_gate"].astype(BF16),
                p[f"ffn{i}_w_up"].astype(BF16), p[f"ffn{i}_w_down"].astype(BF16))

    h1, q, cmp_in, keys, values, gates, xr, xg = _ffn_proj(
        h, ffn_weights(1), row(p["mix_pre_g"]), _pack_w_in(p["w_in"]))
    cmp_k, cmp_v = _compress(cmp_in, p["cmp_k_pe"], p["cmp_k_w1"].astype(BF16),
                             p["cmp_k_w2"].astype(BF16), p["cmp_v_pe"],
                             p["cmp_v_w1"].astype(BF16), p["cmp_v_w2"].astype(BF16))
    attn = _attention(q, cmp_k, cmp_v, keys, values, gates)
    lru = _lru(xr, xg, p["conv_w"], row(p["conv_b"]), _block_diag(p["lru_w_a"]),
               row(p["lru_b_a"]), _block_diag(p["lru_w_x"]), row(p["lru_b_x"]),
               row(p["lru_lambda"]), row(p["lru_out_g"]))
    w_out = p["w_out"].astype(BF16)
    h3 = _mix_ffn(h1.reshape(n, d), attn.reshape(n, ATTN_WIDTH), lru.reshape(n, LRU_WIDTH),
                  row(p["attn_out_g"]), row(p["mix_post_g"]),
                  w_out[:ATTN_WIDTH], w_out[ATTN_WIDTH:], ffn_weights(2))
    return h3.reshape(b, t, d)


_PARAM_NAMES = (
    "ffn1_pre_g", "ffn1_post_g", "ffn1_w_gate", "ffn1_w_up", "ffn1_w_down",
    "mix_pre_g", "mix_post_g", "w_in", "cmp_k_pe", "cmp_k_w1", "cmp_k_w2",
    "cmp_v_pe", "cmp_v_w1", "cmp_v_w2", "conv_w", "conv_b", "lru_w_a", "lru_b_a",
    "lru_w_x", "lru_b_x", "lru_lambda", "attn_out_g", "lru_out_g", "w_out",
    "ffn2_pre_g", "ffn2_post_g", "ffn2_w_gate", "ffn2_w_up", "ffn2_w_down",
)


def kernel(x, ffn1_pre_g, ffn1_post_g, ffn1_w_gate, ffn1_w_up, ffn1_w_down, mix_pre_g, mix_post_g, w_in, cmp_k_pe, cmp_k_w1, cmp_k_w2, cmp_v_pe, cmp_v_w1, cmp_v_w2, conv_w, conv_b, lru_w_a, lru_b_a, lru_w_x, lru_b_x, lru_lambda, attn_out_g, lru_out_g, w_out, ffn2_pre_g, ffn2_post_g, ffn2_w_gate, ffn2_w_up, ffn2_w_down):
    stacked = dict(zip(_PARAM_NAMES, (
        ffn1_pre_g, ffn1_post_g, ffn1_w_gate, ffn1_w_up, ffn1_w_down, mix_pre_g, mix_post_g,
        w_in, cmp_k_pe, cmp_k_w1, cmp_k_w2, cmp_v_pe, cmp_v_w1, cmp_v_w2, conv_w, conv_b,
        lru_w_a, lru_b_a, lru_w_x, lru_b_x, lru_lambda, attn_out_g, lru_out_g, w_out,
        ffn2_pre_g, ffn2_post_g, ffn2_w_gate, ffn2_w_up, ffn2_w_down)))
    h = x
    for layer in range(ffn1_pre_g.shape[0]):
        h = _layer(h, {k: v[layer] for k, v in stacked.items()})
    return h
```

```python
import functools

import numpy as np
import jax
import jax.numpy as jnp
from jax import lax
from jax.experimental import pallas as pl
from jax.experimental.pallas import tpu as pltpu

F32 = jnp.float32
BF16 = jnp.bfloat16

D_MODEL = 1024
N_Q_HEADS = 8
HEAD_DIM = 64
N_KV_GROUPS = 2
HEADS_PER_GROUP = N_Q_HEADS // N_KV_GROUPS
ATTN_WIDTH = N_Q_HEADS * HEAD_DIM
GROUP_WIDTH = HEADS_PER_GROUP * HEAD_DIM
CMP_BLOCK = 32
CMP_STRIDE = 16
CMP_HIDDEN = 256
SEL_BLOCK = 64
SEL_TOPN = 16
WINDOW = 512
Q_BLOCK = 256
LRU_WIDTH = 512
LRU_BLOCKS = 8
CONV_WIDTH = 4
LRU_C = 8.0
D_FF = 2816
NORM_EPS = 1e-6

LANES = 128
SUBLANES = 8
VMEM_LIMIT_BYTES = 56 * 1024 * 1024

NEG_BIG = -1e30
FORCED_SCORE = 3e38
LOG2E = 1.4426950408889634
AUG_DIM = 2 * HEAD_DIM
WIDE_DIM = 4 * HEAD_DIM

FFN_TOKENS = 512
FFN_CHUNK = 256
LRU_TOKENS = 1024
SEL_CHUNK = 256
WIN_KEYS = WINDOW + Q_BLOCK

COL_Q = 0
COL_CMP = COL_Q + ATTN_WIDTH
COL_KV = COL_CMP + 2 * N_KV_GROUPS * HEAD_DIM
COL_GATE = COL_KV + 4 * N_KV_GROUPS * HEAD_DIM
COL_XR = COL_GATE + N_KV_GROUPS * LANES
COL_XG = COL_XR + LRU_WIDTH
PROJ_WIDTH = COL_XG + LRU_WIDTH


def _rms(x, g):
    ms = jnp.mean(x * x, axis=-1, keepdims=True)
    return x * lax.rsqrt(ms + NORM_EPS) * g


def _const_spec(shape):
    nd = len(shape)
    return pl.BlockSpec(shape, lambda *_: (0,) * nd, pipeline_mode=pl.Buffered(1))


def _params(*sem):
    return pltpu.CompilerParams(dimension_semantics=sem, vmem_limit_bytes=VMEM_LIMIT_BYTES)


def _ffn_half_step(x, pre_g_ref, post_g_ref, wg_ref, wu_ref, wd_ref, act_ref):
    xb = _rms(x, pre_g_ref[...]).astype(BF16)
    for c in range(D_FF // FFN_CHUNK):
        sl = slice(c * FFN_CHUNK, (c + 1) * FFN_CHUNK)
        gate = jnp.dot(xb, wg_ref[:, sl], preferred_element_type=F32)
        up = jnp.dot(xb, wu_ref[:, sl], preferred_element_type=F32)
        act_ref[:, sl] = (jax.nn.silu(gate) * up).astype(BF16)
    f = jnp.dot(act_ref[...], wd_ref[...], preferred_element_type=F32)
    return x + 0.5 * _rms(f, post_g_ref[...])


def _ffn_weight_specs():
    return [_const_spec((1, D_MODEL)), _const_spec((1, D_MODEL)), _const_spec((D_MODEL, D_FF)),
            _const_spec((D_MODEL, D_FF)), _const_spec((D_FF, D_MODEL))]


def _key_tail(pos, rows):
    lane = lax.broadcasted_iota(jnp.int32, (rows, HEAD_DIM), 1)
    hi = (pos >> 6).astype(F32)
    lo = (pos & (SEL_BLOCK - 1)).astype(F32)
    return jnp.where(lane < 2, hi, jnp.where(lane < 4, lo, 0.0))


def _wide_value(v):
    return jnp.concatenate([v, v, jnp.ones((v.shape[0], LANES), F32)], axis=1).astype(BF16)


def _ffn_proj_kernel(x_ref, pre_g_ref, post_g_ref, wg_ref, wu_ref, wd_ref, g_ref, w_ref,
                     h_ref, q_ref, cmp_ref, k_ref, v_ref, gate_ref, xr_ref, xg_ref, act_ref):
    tm = x_ref.shape[1]
    h = _ffn_half_step(x_ref[0], pre_g_ref, post_g_ref, wg_ref, wu_ref, wd_ref, act_ref)
    h_ref[0] = h
    hb = _rms(h, g_ref[...]).astype(BF16)
    p = jnp.dot(hb, w_ref[...], preferred_element_type=F32)
    q_ref[0] = (p[:, COL_Q:COL_CMP] * (HEAD_DIM ** -0.5 * LOG2E)).astype(BF16)
    cmp_ref[0] = p[:, COL_CMP:COL_KV]
    pos = pl.program_id(1) * tm + lax.broadcasted_iota(jnp.int32, (tm, 1), 0)
    key_tail = _key_tail(pos, tm)
    for i in range(4 * N_KV_GROUPS):
        lo = COL_KV + i * HEAD_DIM
        x = p[:, lo:lo + HEAD_DIM]
        branch, is_value, gi = i // (2 * N_KV_GROUPS), (i // N_KV_GROUPS) % 2, i % N_KV_GROUPS
        if is_value:
            v_ref[0, branch * N_KV_GROUPS + gi] = _wide_value(x)
        else:
            k_ref[0, branch * N_KV_GROUPS + gi] = jnp.concatenate([x, key_tail], axis=1).astype(BF16)
    gate_ref[0] = jax.nn.sigmoid(p[:, COL_GATE:COL_XR])
    xr_ref[0] = p[:, COL_XR:COL_XG]
    xg_ref[0] = p[:, COL_XG:PROJ_WIDTH]


def _ffn_proj(x, ffn_weights, g, w_packed):
    b, t, _ = x.shape
    tm = FFN_TOKENS

    def tok(width):
        return pl.BlockSpec((1, tm, width), lambda bi, ti: (bi, ti, 0))

    return pl.pallas_call(
        _ffn_proj_kernel,
        grid=(b, t // tm),
        in_specs=[tok(D_MODEL)] + _ffn_weight_specs()
        + [_const_spec((1, D_MODEL)), _const_spec((D_MODEL, PROJ_WIDTH))],
        out_specs=[
            tok(D_MODEL),
            tok(ATTN_WIDTH),
            tok(2 * N_KV_GROUPS * HEAD_DIM),
            pl.BlockSpec((1, 2 * N_KV_GROUPS, tm, AUG_DIM), lambda bi, ti: (bi, 0, ti, 0)),
            pl.BlockSpec((1, 2 * N_KV_GROUPS, tm, WIDE_DIM), lambda bi, ti: (bi, 0, ti, 0)),
            tok(N_KV_GROUPS * LANES),
            tok(LRU_WIDTH),
            tok(LRU_WIDTH),
        ],
        out_shape=[
            jax.ShapeDtypeStruct((b, t, D_MODEL), F32),
            jax.ShapeDtypeStruct((b, t, ATTN_WIDTH), BF16),
            jax.ShapeDtypeStruct((b, t, 2 * N_KV_GROUPS * HEAD_DIM), F32),
            jax.ShapeDtypeStruct((b, 2 * N_KV_GROUPS, t, AUG_DIM), BF16),
            jax.ShapeDtypeStruct((b, 2 * N_KV_GROUPS, t, WIDE_DIM), BF16),
            jax.ShapeDtypeStruct((b, t, N_KV_GROUPS * LANES), F32),
            jax.ShapeDtypeStruct((b, t, LRU_WIDTH), F32),
            jax.ShapeDtypeStruct((b, t, LRU_WIDTH), F32),
        ],
        scratch_shapes=[pltpu.VMEM((tm, D_FF), BF16)],
        compiler_params=_params("parallel", "parallel"),
        name="ffn_proj",
    )(x, *ffn_weights, g, w_packed)


def _compress_kernel(xk_ref, xv_ref, kpe_ref, kw1_ref, kw2_ref, vpe_ref, vw1_ref, vw2_ref,
                     ok_ref, ov_ref):
    n_chunks = xk_ref.shape[1] // CMP_STRIDE
    half = CMP_BLOCK // 2
    kinds = ((xk_ref, kpe_ref, kw1_ref, kw2_ref), (xv_ref, vpe_ref, vw1_ref, vw2_ref))
    top = [jnp.zeros((n_chunks, CMP_HIDDEN), F32) for _ in range(4)]
    bot = [jnp.zeros((n_chunks, CMP_HIDDEN), F32) for _ in range(4)]
    for l in range(half):
        rows = [ref[0, pl.ds(l, n_chunks, stride=CMP_STRIDE), :] for ref in (xk_ref, xv_ref)]
        for s in range(4):
            _, pe_ref, w1_ref, _ = kinds[s // N_KV_GROUPS]
            gi = s % N_KV_GROUPS
            xs = rows[s // N_KV_GROUPS][:, gi * HEAD_DIM:(gi + 1) * HEAD_DIM]
            x_top = (xs + pe_ref[l:l + 1, :]).astype(BF16)
            x_bot = (xs + pe_ref[half + l:half + l + 1, :]).astype(BF16)
            top[s] += jnp.dot(x_top, w1_ref[l * HEAD_DIM:(l + 1) * HEAD_DIM, :],
                              preferred_element_type=F32)
            bot[s] += jnp.dot(x_bot, w1_ref[(half + l) * HEAD_DIM:(half + l + 1) * HEAD_DIM, :],
                              preferred_element_type=F32)
    row = lax.broadcasted_iota(jnp.int32, (n_chunks, HEAD_DIM), 0)
    cmp_end = lax.broadcasted_iota(jnp.int32, (n_chunks, 1), 0) * CMP_STRIDE + (CMP_BLOCK - 1)
    key_tail = _key_tail(cmp_end, n_chunks)
    for s in range(4):
        w2_ref = kinds[s // N_KV_GROUPS][3]
        hidden = top[s] + pltpu.roll(bot[s], n_chunks - 1, axis=0)
        out = jnp.dot(jax.nn.gelu(hidden).astype(BF16), w2_ref[...], preferred_element_type=F32)
        out = jnp.where(row < n_chunks - 1, out, 0.0)
        if s // N_KV_GROUPS == 0:
            ok_ref[0, s % N_KV_GROUPS] = jnp.concatenate([out, key_tail], axis=1).astype(BF16)
        else:
            ov_ref[0, s % N_KV_GROUPS] = _wide_value(out)


def _compress(cmp_in, k_pe, k_w1, k_w2, v_pe, v_w1, v_w2):
    b, t, _ = cmp_in.shape
    n_chunks = t // CMP_STRIDE
    kv_cols = N_KV_GROUPS * HEAD_DIM
    return pl.pallas_call(
        _compress_kernel,
        grid=(b,),
        in_specs=[pl.BlockSpec((1, t, kv_cols), lambda bi: (bi, 0, 0)),
                  pl.BlockSpec((1, t, kv_cols), lambda bi: (bi, 0, 1)),
                  _const_spec(k_pe.shape), _const_spec(k_w1.shape), _const_spec(k_w2.shape),
                  _const_spec(v_pe.shape), _const_spec(v_w1.shape), _const_spec(v_w2.shape)],
        out_specs=[pl.BlockSpec((1, N_KV_GROUPS, n_chunks, AUG_DIM), lambda bi: (bi, 0, 0, 0)),
                   pl.BlockSpec((1, N_KV_GROUPS, n_chunks, WIDE_DIM), lambda bi: (bi, 0, 0, 0))],
        out_shape=[jax.ShapeDtypeStruct((b, N_KV_GROUPS, n_chunks, AUG_DIM), BF16),
                   jax.ShapeDtypeStruct((b, N_KV_GROUPS, n_chunks, WIDE_DIM), BF16)],
        compiler_params=_params("parallel"),
        name="compress",
    )(cmp_in, cmp_in, k_pe, k_w1, k_w2, v_pe, v_w1, v_w2)


_NT = (((1,), (1,)), ((), ()))


def _softmax_numerators(s, bias):
    n_slabs = s.shape[1] // LANES
    probs = []
    for r in range(s.shape[0] // Q_BLOCK):
        sb = s[r * Q_BLOCK:(r + 1) * Q_BLOCK] + bias
        slabs = [sb[:, j * LANES:(j + 1) * LANES] for j in range(n_slabs)]
        lane_max = functools.reduce(jnp.maximum, slabs)
        m = jnp.broadcast_to(jnp.max(lane_max, axis=-1, keepdims=True), (Q_BLOCK, LANES))
        probs.append(jnp.concatenate([jnp.exp2(x - m) for x in slabs], axis=1).astype(BF16))
    return jnp.concatenate(probs, axis=0)


def _pair_scores(q4, k):
    half = q4.shape[0] // 2
    return [lax.dot_general(q4[i * half:(i + 1) * half], k, _NT, preferred_element_type=F32)
            for i in range(2)]


def _pair_attend(scores, v, bias):
    probs = [_softmax_numerators(s, bias) for s in scores]
    outs = [jnp.dot(p, v, preferred_element_type=F32) for p in probs]
    return jnp.concatenate(probs, axis=0), jnp.concatenate(outs, axis=0)


def _for_each_pair(n_pairs, body):
    def two(i, carry):
        body(2 * i, carry)
        return body(2 * i + 1, carry)

    lax.fori_loop(0, n_pairs // 2, two, 0)

    @pl.when(n_pairs % 2 == 1)
    def _():
        body(n_pairs - 1, 0)


def _block_ranks(score):
    n_blocks, width = score.shape
    n_slabs = n_blocks // SUBLANES
    slabs = [score[s * SUBLANES:(s + 1) * SUBLANES, :] for s in range(n_slabs)]
    ranks = [jnp.zeros((SUBLANES, width), F32) for _ in range(n_slabs)]
    sub = lax.broadcasted_iota(jnp.int32, (SUBLANES, width), 0)
    for j in range(n_blocks):
        other = jnp.broadcast_to(score[j:j + 1, :], (SUBLANES, width))
        for s in range(n_slabs):
            if s * SUBLANES > j:
                ahead = jnp.where(other >= slabs[s], 1.0, 0.0)
            elif (s + 1) * SUBLANES - 1 <= j:
                ahead = jnp.where(other > slabs[s], 1.0, 0.0)
            else:
                ahead = jnp.where(sub > j - s * SUBLANES, jnp.where(other >= slabs[s], 1.0, 0.0),
                                  jnp.where(other > slabs[s], 1.0, 0.0))
            ranks[s] = ranks[s] + ahead
    return jnp.concatenate(ranks, axis=0)


def _attn_kernel(q_ref, slope_ref, kc_ref, vc_ref, ks_ref, vs_ref, kw_ref, vw_ref, gate_ref,
                 mcs_t_ref, expand_ref, chunk_of_ref, win_bias_ref, o_ref,
                 q4_ref, chosen_ref, s_ref, m_ref, acc_ref, gate_b_ref, slot_ref):
    qb = pl.program_id(2)
    q0 = qb * Q_BLOCK
    rows = HEADS_PER_GROUP * Q_BLOCK
    n_cmp = kc_ref.shape[2]
    n_sel = expand_ref.shape[1]

    qblk = q_ref[0]
    slope_cols = slope_ref[0]
    q4 = jnp.concatenate(
        [jnp.concatenate([qblk[:, r * HEAD_DIM:(r + 1) * HEAD_DIM],
                          jnp.broadcast_to(slope_cols[r:r + 1, :], (Q_BLOCK, HEAD_DIM))], axis=1)
         for r in range(HEADS_PER_GROUP)], axis=0)
    t_row = q0 + (lax.broadcasted_iota(jnp.int32, (rows, LANES), 0) & (Q_BLOCK - 1))
    tq = q0 + lax.broadcasted_iota(jnp.int32, (Q_BLOCK, 1), 0)

    cmp_end = lax.broadcasted_iota(jnp.int32, (1, n_cmp), 1) * CMP_STRIDE + (CMP_BLOCK - 1)
    pc4, ov = _pair_attend(_pair_scores(q4, kc_ref[0, 0]), vc_ref[0, 0],
                           jnp.where(tq >= cmp_end, 0.0, NEG_BIG))
    o_cmp = jnp.where(t_row >= CMP_BLOCK - 1, ov[:, :LANES] / ov[:, LANES:], 0.0)

    w0 = pl.multiple_of(jnp.maximum(q0 - WINDOW, 0), Q_BLOCK)
    win_scores = _pair_scores(q4, kw_ref[0, 0, pl.ds(w0, WIN_KEYS), :])

    imp_l = lax.dot_general(mcs_t_ref[...], pc4, _NT, preferred_element_type=F32)
    imp = None
    for r in range(HEADS_PER_GROUP):
        cs = slice(r * Q_BLOCK, (r + 1) * Q_BLOCK)
        part = imp_l[:n_sel, cs] / imp_l[n_sel:n_sel + 1, cs]
        imp = part if imp is None else imp + part
    blk = lax.broadcasted_iota(jnp.int32, (n_sel, Q_BLOCK), 0)
    tq_l = q0 + lax.broadcasted_iota(jnp.int32, (n_sel, Q_BLOCK), 1)
    cur = tq_l >> 6
    forced = (blk == 0) | (blk == cur) | (blk == cur - 1)
    valid = blk * SEL_BLOCK <= tq_l
    score = jnp.where(forced, FORCED_SCORE, jnp.where(valid, imp, -1.0))
    rank = _block_ranks(score)
    chosen_t = jnp.where(valid, jnp.where(rank < float(min(SEL_TOPN, n_sel)), 1.0, 0.0), 0.0)
    chosen_ref[...] = chosen_t.T.astype(BF16)
    q4_ref[...] = q4

    n_chunks = expand_ref.shape[0]
    per_chunk = jnp.dot(chunk_of_ref[...], chosen_t.astype(BF16), preferred_element_type=F32)
    live = jnp.max(per_chunk, axis=1, keepdims=True) > 0.0
    weight = (1 << lax.broadcasted_iota(jnp.int32, (n_chunks, 1), 0)).astype(F32)
    live_bits = jnp.sum(jnp.where(live, weight, 0.0)).astype(jnp.int32)

    n_live = 0
    for c in range(n_chunks):
        slot_ref[n_live] = c
        n_live = n_live + ((live_bits >> c) & 1)
    slot_ref[n_live] = 0
    n_pairs = (n_live + 1) // 2

    vwin = vw_ref[0, 0, pl.ds(w0, WIN_KEYS), :]
    bias_w = win_bias_ref[jnp.minimum(qb, WINDOW // Q_BLOCK)]
    _, wv = _pair_attend(win_scores, vwin, bias_w)
    o_win = wv[:, :LANES] / wv[:, LANES:]

    for r in range(HEADS_PER_GROUP):
        for j in range(3):
            gate_b_ref[3 * r + j] = jnp.broadcast_to(gate_ref[0, :, 3 * r + j:3 * r + j + 1],
                                                     (Q_BLOCK, LANES))

    m_ref[...] = jnp.full(m_ref.shape, NEG_BIG, F32)

    def score_body(i, carry):
        lane_max = [None] * HEADS_PER_GROUP
        starts, biases = [], []
        for half in range(2):
            slot = 2 * i + half
            c = slot_ref[slot]
            k0 = pl.multiple_of(c * SEL_CHUNK, SEL_CHUNK)
            hit = jnp.dot(chosen_ref[...], expand_ref[c], preferred_element_type=F32)
            pos = k0 + lax.broadcasted_iota(jnp.int32, (1, SEL_CHUNK), 1)
            bias = jnp.where(pos <= tq, (hit - 1.0) * (-NEG_BIG), NEG_BIG)
            biases.append(jnp.where(slot < n_live, bias, NEG_BIG))
            starts.append(k0)
        for half in range(2):
            slot = 2 * i + half
            bias = biases[half]
            k = ks_ref[0, 0, pl.ds(starts[half], SEL_CHUNK), :]
            s4c = lax.dot_general(q4_ref[...], k, _NT, preferred_element_type=F32)
            for r in range(HEADS_PER_GROUP):
                rs = slice(r * Q_BLOCK, (r + 1) * Q_BLOCK)
                s = s4c[rs] + bias
                s_ref[slot, rs, :] = s
                mx = jnp.maximum(s[:, :LANES], s[:, LANES:])
                lane_max[r] = mx if lane_max[r] is None else jnp.maximum(lane_max[r], mx)
        for r in range(HEADS_PER_GROUP):
            rs = slice(r * Q_BLOCK, (r + 1) * Q_BLOCK)
            m_ref[rs, :] = jnp.maximum(m_ref[rs, :], lane_max[r])
        return carry

    _for_each_pair(n_pairs, score_body)
    m_ref[...] = jnp.broadcast_to(jnp.max(m_ref[...], axis=-1, keepdims=True), m_ref.shape)

    acc_ref[...] = jnp.zeros(acc_ref.shape, F32)

    def value_body(i, carry):
        pv = None
        for half in range(2):
            slot = 2 * i + half
            v0 = pl.multiple_of(slot_ref[slot] * SEL_CHUNK, SEL_CHUNK)
            v = vs_ref[0, 0, pl.ds(v0, SEL_CHUNK), :]
            p = jnp.concatenate([jnp.exp2(s_ref[slot, :, :LANES] - m_ref[...]),
                                 jnp.exp2(s_ref[slot, :, LANES:] - m_ref[...])],
                                axis=1).astype(BF16)
            d = jnp.dot(p, v, preferred_element_type=F32)
            pv = d if pv is None else pv + d
        acc_ref[...] += pv
        return carry

    _for_each_pair(n_pairs, value_body)
    o_sel = acc_ref[:, :LANES] / acc_ref[:, LANES:]

    lane = lax.broadcasted_iota(jnp.int32, (Q_BLOCK, LANES), 1)
    gated = []
    for r in range(HEADS_PER_GROUP):
        rs = slice(r * Q_BLOCK, (r + 1) * Q_BLOCK)
        gated.append(gate_b_ref[3 * r] * o_cmp[rs] + gate_b_ref[3 * r + 1] * o_sel[rs]
                     + gate_b_ref[3 * r + 2] * o_win[rs])
    o_ref[0] = jnp.concatenate(
        [jnp.where(lane < HEAD_DIM, gated[2 * i], gated[2 * i + 1])
         for i in range(HEADS_PER_GROUP // 2)], axis=-1)


def _attention(q, cmp_k, cmp_v, keys, values, gates):
    b, t, _ = q.shape
    n_cmp = cmp_k.shape[2]
    n_sel = t // SEL_BLOCK
    n_real_cmp = (t - CMP_BLOCK) // CMP_STRIDE + 1

    cs = np.arange(n_cmp)[None, :] * CMP_STRIDE
    js = np.arange(n_sel)[:, None] * SEL_BLOCK
    ov = np.minimum(cs + CMP_BLOCK, js + SEL_BLOCK) - np.maximum(cs, js)
    mcs_t = np.zeros((n_sel + SUBLANES, n_cmp), np.float32)
    mcs_t[:n_sel] = np.maximum(ov, 0).astype(np.float32) / CMP_BLOCK
    mcs_t[:n_sel, n_real_cmp:] = 0.0
    mcs_t[n_sel] = 1.0
    key_blk = (np.arange(t) // SEL_BLOCK).reshape(t // SEL_CHUNK, 1, SEL_CHUNK)
    expand = (key_blk == np.arange(n_sel)[None, :, None]).astype(np.float32)
    n_chunks = t // SEL_CHUNK
    chunk_of = (np.arange(n_sel)[None, :] // (SEL_CHUNK // SEL_BLOCK)
                == np.arange(n_chunks)[:, None]).astype(np.float32)
    rows = HEADS_PER_GROUP * Q_BLOCK
    dist = (np.arange(WINDOW // Q_BLOCK + 1)[:, None, None] * Q_BLOCK
            + np.arange(Q_BLOCK)[None, :, None] - np.arange(WIN_KEYS)[None, None, :])
    win_bias = np.where((dist >= 0) & (dist < WINDOW), 0.0, NEG_BIG).astype(np.float32)

    slope_l2 = LOG2E * np.power(2.0, -8.0 * np.arange(1, N_Q_HEADS + 1) / N_Q_HEADS)
    hi = slope_l2.astype(BF16).astype(np.float64)
    lo = (slope_l2 - hi).astype(BF16).astype(np.float64)
    cols = np.zeros((N_KV_GROUPS, SUBLANES, HEAD_DIM), np.float32)
    heads = cols[:, :HEADS_PER_GROUP].reshape(N_Q_HEADS, HEAD_DIM)
    heads[:, 0], heads[:, 1], heads[:, 2], heads[:, 3] = SEL_BLOCK * hi, SEL_BLOCK * lo, hi, lo
    cols[:, :HEADS_PER_GROUP] = heads.reshape(N_KV_GROUPS, HEADS_PER_GROUP, HEAD_DIM)

    def kv_spec(branch, rows, width):
        return pl.BlockSpec((1, 1, rows, width),
                            lambda bi, gi, qi, branch=branch: (bi, branch * N_KV_GROUPS + gi, 0, 0))

    return pl.pallas_call(
        _attn_kernel,
        grid=(b, N_KV_GROUPS, t // Q_BLOCK),
        in_specs=[
            pl.BlockSpec((1, Q_BLOCK, GROUP_WIDTH), lambda bi, gi, qi: (bi, qi, gi)),
            pl.BlockSpec((1, SUBLANES, HEAD_DIM), lambda bi, gi, qi: (gi, 0, 0)),
            kv_spec(0, n_cmp, AUG_DIM), kv_spec(0, n_cmp, WIDE_DIM),
            kv_spec(0, t, AUG_DIM), kv_spec(0, t, WIDE_DIM),
            kv_spec(1, t, AUG_DIM), kv_spec(1, t, WIDE_DIM),
            pl.BlockSpec((1, Q_BLOCK, LANES), lambda bi, gi, qi: (bi, qi, gi)),
            _const_spec(mcs_t.shape), _const_spec(expand.shape), _const_spec(chunk_of.shape),
            _const_spec(win_bias.shape),
        ],
        out_specs=pl.BlockSpec((1, Q_BLOCK, GROUP_WIDTH), lambda bi, gi, qi: (bi, qi, gi)),
        out_shape=jax.ShapeDtypeStruct((b, t, ATTN_WIDTH), F32),
        scratch_shapes=[
            pltpu.VMEM((rows, AUG_DIM), BF16),
            pltpu.VMEM((Q_BLOCK, n_sel), BF16),
            pltpu.VMEM((n_chunks, rows, SEL_CHUNK), F32),
            pltpu.VMEM((rows, LANES), F32),
            pltpu.VMEM((rows, WIDE_DIM), F32),
            pltpu.VMEM((3 * HEADS_PER_GROUP, Q_BLOCK, LANES), F32),
            pltpu.SMEM((n_chunks + 1,), jnp.int32),
        ],
        compiler_params=_params("parallel", "parallel", "arbitrary"),
        name="nsa_attn",
    )(q, jnp.asarray(cols, BF16), cmp_k, cmp_v, keys, values, keys, values, gates,
      jnp.asarray(mcs_t, BF16), jnp.asarray(expand, BF16), jnp.asarray(chunk_of, BF16),
      jnp.asarray(win_bias))


def _lru_kernel(xr_ref, xg_ref, cw_ref, cb_ref, wa_ref, ba_ref, wx_ref, bx_ref, lam_ref,
                g_ref, o_ref, xs_ref, a_ref, b_ref, h_ref):
    ti = pl.program_id(1)
    tt = xr_ref.shape[1]

    @pl.when(ti == 0)
    def _():
        xs_ref[...] = jnp.zeros_like(xs_ref)
        h_ref[...] = jnp.zeros_like(h_ref)

    n_groups = tt // SUBLANES
    as_groups = lambda v: v.reshape(n_groups, SUBLANES, LRU_WIDTH)
    sub = lax.broadcasted_iota(jnp.int32, (n_groups, SUBLANES, LRU_WIDTH), 1)

    x = as_groups(xr_ref[0])
    last = xs_ref[...]
    xc = cb_ref[...] + cw_ref[CONV_WIDTH - 1:CONV_WIDTH, :] * x
    for j in range(CONV_WIDTH - 1):
        lag = CONV_WIDTH - 1 - j
        rot = pltpu.roll(x, lag, axis=1)
        rot_before = jnp.concatenate([pltpu.roll(last, lag, axis=0)[None], rot[:-1]], axis=0)
        xc = xc + cw_ref[j:j + 1, :] * jnp.where(sub < lag, rot_before, rot)
    xs_ref[...] = x[n_groups - 1]
    xc = xc.reshape(tt, LRU_WIDTH)

    xb = xc.astype(BF16)
    r = jax.nn.sigmoid(jnp.dot(xb, wa_ref[...], preferred_element_type=F32) + ba_ref[...])
    i = jax.nn.sigmoid(jnp.dot(xb, wx_ref[...], preferred_element_type=F32) + bx_ref[...])
    neg_lam = -lam_ref[...]
    softplus = jnp.maximum(neg_lam, 0.0) + jnp.log1p(jnp.exp(-jnp.abs(neg_lam)))
    log_a = -LRU_C * r * softplus
    a = jnp.exp(log_a)
    th = jnp.tanh(log_a)
    mult = jnp.sqrt(jnp.maximum(-2.0 * th / (1.0 - th), 0.0))
    b = mult * i * xc

    a, b = as_groups(a), as_groups(b)
    for s in (1, 2, 4):
        ok = sub >= s
        a_prev = jnp.where(ok, pltpu.roll(a, s, axis=1), 1.0)
        b_prev = jnp.where(ok, pltpu.roll(b, s, axis=1), 0.0)
        b = a * b_prev + b
        a = a * a_prev
    a_ref[...] = a.reshape(tt, LRU_WIDTH)
    b_ref[...] = b.reshape(tt, LRU_WIDTH)

    def group_body(k, h):
        r0 = pl.multiple_of(k * SUBLANES, SUBLANES)
        h8 = a_ref[pl.ds(r0, SUBLANES), :] * h + b_ref[pl.ds(r0, SUBLANES), :]
        b_ref[pl.ds(r0, SUBLANES), :] = h8
        return jnp.broadcast_to(h8[SUBLANES - 1:SUBLANES, :], (SUBLANES, LRU_WIDTH))

    h_ref[...] = lax.fori_loop(0, tt // SUBLANES, group_body, h_ref[...])
    out = b_ref[...] * jax.nn.gelu(xg_ref[0])
    o_ref[0] = _rms(out, g_ref[...]).astype(BF16)


def _lru(xr, xg, conv_w, conv_b, wa, ba, wx, bx, lam, g):
    b, t, c = xr.shape
    tt = LRU_TOKENS
    tok = pl.BlockSpec((1, tt, c), lambda bi, ti: (bi, ti, 0))
    vec = _const_spec((1, c))
    return pl.pallas_call(
        _lru_kernel,
        grid=(b, t // tt),
        in_specs=[tok, tok, _const_spec((CONV_WIDTH, c)), vec, _const_spec((c, c)), vec,
                  _const_spec((c, c)), vec, vec, vec],
        out_specs=tok,
        out_shape=jax.ShapeDtypeStruct((b, t, c), BF16),
        scratch_shapes=[pltpu.VMEM((SUBLANES, c), F32), pltpu.VMEM((tt, c), F32),
                        pltpu.VMEM((tt, c), F32), pltpu.VMEM((SUBLANES, c), F32)],
        compiler_params=_params("parallel", "arbitrary"),
        name="rg_lru",
    )(xr, xg, conv_w, conv_b, wa, ba, wx, bx, lam, g)


def _mix_ffn_kernel(h_ref, attn_ref, lru_ref, ga_ref, mix_g_ref, wo_a_ref, wo_l_ref,
                    pre_g_ref, post_g_ref, wg_ref, wu_ref, wd_ref, o_ref, act_ref):
    ya = _rms(attn_ref[...], ga_ref[...]).astype(BF16)
    m = (jnp.dot(ya, wo_a_ref[...], preferred_element_type=F32)
         + jnp.dot(lru_ref[...], wo_l_ref[...], preferred_element_type=F32))
    h = h_ref[...] + _rms(m, mix_g_ref[...])
    o_ref[...] = _ffn_half_step(h, pre_g_ref, post_g_ref, wg_ref, wu_ref, wd_ref, act_ref)


def _mix_ffn(h, attn, lru, attn_g, mix_g, wo_a, wo_l, ffn_weights):
    n = h.shape[0]
    tm = FFN_TOKENS

    def tok(width):
        return pl.BlockSpec((tm, width), lambda i: (i, 0))

    return pl.pallas_call(
        _mix_ffn_kernel,
        grid=(n // tm,),
        in_specs=[tok(D_MODEL), tok(ATTN_WIDTH), tok(LRU_WIDTH), _const_spec((1, ATTN_WIDTH)),
                  _const_spec((1, D_MODEL)), _const_spec((ATTN_WIDTH, D_MODEL)),
                  _const_spec((LRU_WIDTH, D_MODEL))] + _ffn_weight_specs(),
        out_specs=tok(D_MODEL),
        out_shape=jax.ShapeDtypeStruct((n, D_MODEL), F32),
        scratch_shapes=[pltpu.VMEM((tm, D_FF), BF16)],
        compiler_params=_params("parallel"),
        name="mix_ffn",
    )(h, attn, lru, attn_g, mix_g, wo_a, wo_l, *ffn_weights)


def _pack_w_in(w_in):
    kv_cols = N_KV_GROUPS * HEAD_DIM
    gate_lo = ATTN_WIDTH + 6 * kv_cols
    gate_hi = gate_lo + 3 * N_Q_HEADS
    per_group = 3 * HEADS_PER_GROUP
    pad = jnp.zeros((w_in.shape[0], LANES - per_group), w_in.dtype)
    gate_slabs = []
    for gi in range(N_KV_GROUPS):
        gate_slabs += [w_in[:, gate_lo + gi * per_group:gate_lo + (gi + 1) * per_group], pad]
    return jnp.concatenate([w_in[:, :gate_lo]] + gate_slabs + [w_in[:, gate_hi:]], axis=1).astype(BF16)


def _block_diag(w):
    nb, d, e = w.shape
    eye = jnp.eye(nb, dtype=w.dtype)
    return jnp.einsum("nde,nm->ndme", w, eye).reshape(nb * d, nb * e).astype(BF16)


def _layer(h, p):
    b, t, d = h.shape
    n = b * t
    row = lambda v: v.reshape(1, -1)

    def ffn_weights(i):
        return (row(p[f"ffn{i}_pre_g"]), row(p[f"ffn{i}_post_g"]), p[f"ffn{i}_w_gate"].astype(BF16),
                p[f"ffn{i}_w_up"].astype(BF16), p[f"ffn{i}_w_down"].astype(BF16))

    h1, q, cmp_in, keys, values, gates, xr, xg = _ffn_proj(
        h, ffn_weights(1), row(p["mix_pre_g"]), _pack_w_in(p["w_in"]))
    cmp_k, cmp_v = _compress(cmp_in, p["cmp_k_pe"], p["cmp_k_w1"].astype(BF16),
                             p["cmp_k_w2"].astype(BF16), p["cmp_v_pe"],
                             p["cmp_v_w1"].astype(BF16), p["cmp_v_w2"].astype(BF16))
    attn = _attention(q, cmp_k, cmp_v, keys, values, gates)
    lru = _lru(xr, xg, p["conv_w"], row(p["conv_b"]), _block_diag(p["lru_w_a"]),
               row(p["lru_b_a"]), _block_diag(p["lru_w_x"]), row(p["lru_b_x"]),
               row(p["lru_lambda"]), row(p["lru_out_g"]))
    w_out = p["w_out"].astype(BF16)
    h3 = _mix_ffn(h1.reshape(n, d), attn.reshape(n, ATTN_WIDTH), lru.reshape(n, LRU_WIDTH),
                  row(p["attn_out_g"]), row(p["mix_post_g"]),
                  w_out[:ATTN_WIDTH], w_out[ATTN_WIDTH:], ffn_weights(2))
    return h3.reshape(b, t, d)


_PARAM_NAMES = (
    "ffn1_pre_g", "ffn1_post_g", "ffn1_w_gate", "ffn1_w_up", "ffn1_w_down",
    "mix_pre_g", "mix_post_g", "w_in", "cmp_k_pe", "cmp_k_w1", "cmp_k_w2",
    "cmp_v_pe", "cmp_v_w1", "cmp_v_w2", "conv_w", "conv_b", "lru_w_a", "lru_b_a",
    "lru_w_x", "lru_b_x", "lru_lambda", "attn_out_g", "lru_out_g", "w_out",
    "ffn2_pre_g", "ffn2_post_g", "ffn2_w_gate", "ffn2_w_up", "ffn2_w_down",
)


def kernel(x, ffn1_pre_g, ffn1_post_g, ffn1_w_gate, ffn1_w_up, ffn1_w_down, mix_pre_g, mix_post_g, w_in, cmp_k_pe, cmp_k_w1, cmp_k_w2, cmp_v_pe, cmp_v_w1, cmp_v_w2, conv_w, conv_b, lru_w_a, lru_b_a, lru_w_x, lru_b_x, lru_lambda, attn_out_g, lru_out_g, w_out, ffn2_pre_g, ffn2_post_g, ffn2_w_gate, ffn2_w_up, ffn2_w_down):
    stacked = dict(zip(_PARAM_NAMES, (
        ffn1_pre_g, ffn1_post_g, ffn1_w_gate, ffn1_w_up, ffn1_w_down, mix_pre_g, mix_post_g,
        w_in, cmp_k_pe, cmp_k_w1, cmp_k_w2, cmp_v_pe, cmp_v_w1, cmp_v_w2, conv_w, conv_b,
        lru_w_a, lru_b_a, lru_w_x, lru_b_x, lru_lambda, attn_out_g, lru_out_g, w_out,
        ffn2_pre_g, ffn2_post_g, ffn2_w_gate, ffn2_w_up, ffn2_w_down)))
    h = x
    for layer in range(ffn1_pre_g.shape[0]):
        h = _layer(h, {k: v[layer] for k, v in stacked.items()})
    return h
```

```python
import functools

import numpy as np
import jax
import jax.numpy as jnp
from jax import lax
from jax.experimental import pallas as pl
from jax.experimental.pallas import tpu as pltpu

F32 = jnp.float32
BF16 = jnp.bfloat16

D_MODEL = 1024
N_Q_HEADS = 8
HEAD_DIM = 64
N_KV_GROUPS = 2
HEADS_PER_GROUP = N_Q_HEADS // N_KV_GROUPS
ATTN_WIDTH = N_Q_HEADS * HEAD_DIM
GROUP_WIDTH = HEADS_PER_GROUP * HEAD_DIM
CMP_BLOCK = 32
CMP_STRIDE = 16
CMP_HIDDEN = 256
SEL_BLOCK = 64
SEL_TOPN = 16
WINDOW = 512
Q_BLOCK = 256
LRU_WIDTH = 512
LRU_BLOCKS = 8
CONV_WIDTH = 4
LRU_C = 8.0
D_FF = 2816
NORM_EPS = 1e-6

LANES = 128
SUBLANES = 8
VMEM_LIMIT_BYTES = 56 * 1024 * 1024

NEG_BIG = -1e30
FORCED_SCORE = 3e38
LOG2E = 1.4426950408889634
AUG_DIM = 2 * HEAD_DIM
WIDE_DIM = 4 * HEAD_DIM

FFN_TOKENS = 512
FFN_CHUNK = 256
LRU_TOKENS = 1024
SEL_CHUNK = 256
WIN_KEYS = WINDOW + Q_BLOCK

COL_Q = 0
COL_CMP = COL_Q + ATTN_WIDTH
COL_KV = COL_CMP + 2 * N_KV_GROUPS * HEAD_DIM
COL_GATE = COL_KV + 4 * N_KV_GROUPS * HEAD_DIM
COL_XR = COL_GATE + N_KV_GROUPS * LANES
COL_XG = COL_XR + LRU_WIDTH
PROJ_WIDTH = COL_XG + LRU_WIDTH


def _rms(x, g):
    ms = jnp.mean(x * x, axis=-1, keepdims=True)
    return x * lax.rsqrt(ms + NORM_EPS) * g


def _const_spec(shape):
    nd = len(shape)
    return pl.BlockSpec(shape, lambda *_: (0,) * nd, pipeline_mode=pl.Buffered(1))


def _params(*sem):
    return pltpu.CompilerParams(dimension_semantics=sem, vmem_limit_bytes=VMEM_LIMIT_BYTES)


def _ffn_half_step(x, pre_g_ref, post_g_ref, wg_ref, wu_ref, wd_ref, act_ref):
    xb = _rms(x, pre_g_ref[...]).astype(BF16)
    for c in range(D_FF // FFN_CHUNK):
        sl = slice(c * FFN_CHUNK, (c + 1) * FFN_CHUNK)
        gate = jnp.dot(xb, wg_ref[:, sl], preferred_element_type=F32)
        up = jnp.dot(xb, wu_ref[:, sl], preferred_element_type=F32)
        act_ref[:, sl] = (jax.nn.silu(gate) * up).astype(BF16)
    f = jnp.dot(act_ref[...], wd_ref[...], preferred_element_type=F32)
    return x + 0.5 * _rms(f, post_g_ref[...])


def _ffn_weight_specs():
    return [_const_spec((1, D_MODEL)), _const_spec((1, D_MODEL)), _const_spec((D_MODEL, D_FF)),
            _const_spec((D_MODEL, D_FF)), _const_spec((D_FF, D_MODEL))]


def _key_tail(pos, rows):
    lane = lax.broadcasted_iota(jnp.int32, (rows, HEAD_DIM), 1)
    hi = (pos >> 6).astype(F32)
    lo = (pos & (SEL_BLOCK - 1)).astype(F32)
    return jnp.where(lane < 2, hi, jnp.where(lane < 4, lo, 0.0))


def _wide_value(v):
    return jnp.concatenate([v, v, jnp.ones((v.shape[0], LANES), F32)], axis=1).astype(BF16)


def _ffn_proj_kernel(x_ref, pre_g_ref, post_g_ref, wg_ref, wu_ref, wd_ref, g_ref, w_ref,
                     h_ref, q_ref, cmp_ref, k_ref, v_ref, gate_ref, xr_ref, xg_ref, act_ref):
    tm = x_ref.shape[1]
    h = _ffn_half_step(x_ref[0], pre_g_ref, post_g_ref, wg_ref, wu_ref, wd_ref, act_ref)
    h_ref[0] = h
    hb = _rms(h, g_ref[...]).astype(BF16)
    p = jnp.dot(hb, w_ref[...], preferred_element_type=F32)
    q_ref[0] = (p[:, COL_Q:COL_CMP] * (HEAD_DIM ** -0.5 * LOG2E)).astype(BF16)
    cmp_ref[0] = p[:, COL_CMP:COL_KV]
    pos = pl.program_id(1) * tm + lax.broadcasted_iota(jnp.int32, (tm, 1), 0)
    key_tail = _key_tail(pos, tm)
    for i in range(4 * N_KV_GROUPS):
        lo = COL_KV + i * HEAD_DIM
        x = p[:, lo:lo + HEAD_DIM]
        branch, is_value, gi = i // (2 * N_KV_GROUPS), (i // N_KV_GROUPS) % 2, i % N_KV_GROUPS
        if is_value:
            v_ref[0, branch * N_KV_GROUPS + gi] = _wide_value(x)
        else:
            k_ref[0, branch * N_KV_GROUPS + gi] = jnp.concatenate([x, key_tail], axis=1).astype(BF16)
    gate_ref[0] = jax.nn.sigmoid(p[:, COL_GATE:COL_XR])
    xr_ref[0] = p[:, COL_XR:COL_XG]
    xg_ref[0] = p[:, COL_XG:PROJ_WIDTH]


def _ffn_proj(x, ffn_weights, g, w_packed):
    b, t, _ = x.shape
    tm = FFN_TOKENS

    def tok(width):
        return pl.BlockSpec((1, tm, width), lambda bi, ti: (bi, ti, 0))

    return pl.pallas_call(
        _ffn_proj_kernel,
        grid=(b, t // tm),
        in_specs=[tok(D_MODEL)] + _ffn_weight_specs()
        + [_const_spec((1, D_MODEL)), _const_spec((D_MODEL, PROJ_WIDTH))],
        out_specs=[
            tok(D_MODEL),
            tok(ATTN_WIDTH),
            tok(2 * N_KV_GROUPS * HEAD_DIM),
            pl.BlockSpec((1, 2 * N_KV_GROUPS, tm, AUG_DIM), lambda bi, ti: (bi, 0, ti, 0)),
            pl.BlockSpec((1, 2 * N_KV_GROUPS, tm, WIDE_DIM), lambda bi, ti: (bi, 0, ti, 0)),
            tok(N_KV_GROUPS * LANES),
            tok(LRU_WIDTH),
            tok(LRU_WIDTH),
        ],
        out_shape=[
            jax.ShapeDtypeStruct((b, t, D_MODEL), F32),
            jax.ShapeDtypeStruct((b, t, ATTN_WIDTH), BF16),
            jax.ShapeDtypeStruct((b, t, 2 * N_KV_GROUPS * HEAD_DIM), F32),
            jax.ShapeDtypeStruct((b, 2 * N_KV_GROUPS, t, AUG_DIM), BF16),
            jax.ShapeDtypeStruct((b, 2 * N_KV_GROUPS, t, WIDE_DIM), BF16),
            jax.ShapeDtypeStruct((b, t, N_KV_GROUPS * LANES), F32),
            jax.ShapeDtypeStruct((b, t, LRU_WIDTH), F32),
            jax.ShapeDtypeStruct((b, t, LRU_WIDTH), F32),
        ],
        scratch_shapes=[pltpu.VMEM((tm, D_FF), BF16)],
        compiler_params=_params("parallel", "parallel"),
        name="ffn_proj",
    )(x, *ffn_weights, g, w_packed)


def _compress_kernel(xk_ref, xv_ref, kpe_ref, kw1_ref, kw2_ref, vpe_ref, vw1_ref, vw2_ref,
                     ok_ref, ov_ref):
    n_chunks = xk_ref.shape[1] // CMP_STRIDE
    half = CMP_BLOCK // 2
    kinds = ((xk_ref, kpe_ref, kw1_ref, kw2_ref), (xv_ref, vpe_ref, vw1_ref, vw2_ref))
    top = [jnp.zeros((n_chunks, CMP_HIDDEN), F32) for _ in range(4)]
    bot = [jnp.zeros((n_chunks, CMP_HIDDEN), F32) for _ in range(4)]
    for l in range(half):
        rows = [ref[0, pl.ds(l, n_chunks, stride=CMP_STRIDE), :] for ref in (xk_ref, xv_ref)]
        for s in range(4):
            _, pe_ref, w1_ref, _ = kinds[s // N_KV_GROUPS]
            gi = s % N_KV_GROUPS
            xs = rows[s // N_KV_GROUPS][:, gi * HEAD_DIM:(gi + 1) * HEAD_DIM]
            x_top = (xs + pe_ref[l:l + 1, :]).astype(BF16)
            x_bot = (xs + pe_ref[half + l:half + l + 1, :]).astype(BF16)
            top[s] += jnp.dot(x_top, w1_ref[l * HEAD_DIM:(l + 1) * HEAD_DIM, :],
                              preferred_element_type=F32)
            bot[s] += jnp.dot(x_bot, w1_ref[(half + l) * HEAD_DIM:(half + l + 1) * HEAD_DIM, :],
                              preferred_element_type=F32)
    row = lax.broadcasted_iota(jnp.int32, (n_chunks, HEAD_DIM), 0)
    cmp_end = lax.broadcasted_iota(jnp.int32, (n_chunks, 1), 0) * CMP_STRIDE + (CMP_BLOCK - 1)
    key_tail = _key_tail(cmp_end, n_chunks)
    for s in range(4):
        w2_ref = kinds[s // N_KV_GROUPS][3]
        hidden = top[s] + pltpu.roll(bot[s], n_chunks - 1, axis=0)
        out = jnp.dot(jax.nn.gelu(hidden).astype(BF16), w2_ref[...], preferred_element_type=F32)
        out = jnp.where(row < n_chunks - 1, out, 0.0)
        if s // N_KV_GROUPS == 0:
            ok_ref[0, s % N_KV_GROUPS] = jnp.concatenate([out, key_tail], axis=1).astype(BF16)
        else:
            ov_ref[0, s % N_KV_GROUPS] = _wide_value(out)


def _compress(cmp_in, k_pe, k_w1, k_w2, v_pe, v_w1, v_w2):
    b, t, _ = cmp_in.shape
    n_chunks = t // CMP_STRIDE
    kv_cols = N_KV_GROUPS * HEAD_DIM
    return pl.pallas_call(
        _compress_kernel,
        grid=(b,),
        in_specs=[pl.BlockSpec((1, t, kv_cols), lambda bi: (bi, 0, 0)),
                  pl.BlockSpec((1, t, kv_cols), lambda bi: (bi, 0, 1)),
                  _const_spec(k_pe.shape), _const_spec(k_w1.shape), _const_spec(k_w2.shape),
                  _const_spec(v_pe.shape), _const_spec(v_w1.shape), _const_spec(v_w2.shape)],
        out_specs=[pl.BlockSpec((1, N_KV_GROUPS, n_chunks, AUG_DIM), lambda bi: (bi, 0, 0, 0)),
                   pl.BlockSpec((1, N_KV_GROUPS, n_chunks, WIDE_DIM), lambda bi: (bi, 0, 0, 0))],
        out_shape=[jax.ShapeDtypeStruct((b, N_KV_GROUPS, n_chunks, AUG_DIM), BF16),
                   jax.ShapeDtypeStruct((b, N_KV_GROUPS, n_chunks, WIDE_DIM), BF16)],
        compiler_params=_params("parallel"),
        name="compress",
    )(cmp_in, cmp_in, k_pe, k_w1, k_w2, v_pe, v_w1, v_w2)


_NT = (((1,), (1,)), ((), ()))


def _softmax_numerators(s, bias):
    n_slabs = s.shape[1] // LANES
    probs = []
    for r in range(s.shape[0] // Q_BLOCK):
        sb = s[r * Q_BLOCK:(r + 1) * Q_BLOCK] + bias
        slabs = [sb[:, j * LANES:(j + 1) * LANES] for j in range(n_slabs)]
        lane_max = functools.reduce(jnp.maximum, slabs)
        m = jnp.broadcast_to(jnp.max(lane_max, axis=-1, keepdims=True), (Q_BLOCK, LANES))
        probs.append(jnp.concatenate([jnp.exp2(x - m) for x in slabs], axis=1).astype(BF16))
    return jnp.concatenate(probs, axis=0)


def _pair_scores(q4, k):
    half = q4.shape[0] // 2
    return [lax.dot_general(q4[i * half:(i + 1) * half], k, _NT, preferred_element_type=F32)
            for i in range(2)]


def _pair_attend(scores, v, bias):
    probs = [_softmax_numerators(s, bias) for s in scores]
    outs = [jnp.dot(p, v, preferred_element_type=F32) for p in probs]
    return jnp.concatenate(probs, axis=0), jnp.concatenate(outs, axis=0)


def _for_each_group(n, body):
    def four(i, carry):
        body(4 * i, 2)
        body(4 * i + 2, 2)
        return carry

    lax.fori_loop(0, n // 4, four, 0)
    rest = (n // 4) * 4

    @pl.when(n - rest >= 2)
    def _():
        body(rest, 2)

    @pl.when((n - rest) % 2 == 1)
    def _():
        body(n - 1, 1)


def _block_ranks(score):
    n_blocks, width = score.shape
    n_slabs = n_blocks // SUBLANES
    slabs = [score[s * SUBLANES:(s + 1) * SUBLANES, :] for s in range(n_slabs)]
    ranks = [jnp.zeros((SUBLANES, width), F32) for _ in range(n_slabs)]
    sub = lax.broadcasted_iota(jnp.int32, (SUBLANES, width), 0)
    for j in range(n_blocks):
        other = jnp.broadcast_to(score[j:j + 1, :], (SUBLANES, width))
        for s in range(n_slabs):
            if s * SUBLANES > j:
                ahead = jnp.where(other >= slabs[s], 1.0, 0.0)
            elif (s + 1) * SUBLANES - 1 <= j:
                ahead = jnp.where(other > slabs[s], 1.0, 0.0)
            else:
                ahead = jnp.where(sub > j - s * SUBLANES, jnp.where(other >= slabs[s], 1.0, 0.0),
                                  jnp.where(other > slabs[s], 1.0, 0.0))
            ranks[s] = ranks[s] + ahead
    return jnp.concatenate(ranks, axis=0)


def _attn_kernel(q_ref, slope_ref, kc_ref, vc_ref, ks_ref, vs_ref, kw_ref, vw_ref, gate_ref,
                 mcs_t_ref, expand_ref, chunk_of_ref, win_bias_ref, o_ref,
                 q4_ref, chosen_ref, s_ref, m_ref, acc_ref, gate_b_ref, slot_ref):
    qb = pl.program_id(2)
    q0 = qb * Q_BLOCK
    rows = HEADS_PER_GROUP * Q_BLOCK
    n_cmp = kc_ref.shape[2]
    n_sel = expand_ref.shape[1]

    qblk = q_ref[0]
    slope_cols = slope_ref[0]
    q4 = jnp.concatenate(
        [jnp.concatenate([qblk[:, r * HEAD_DIM:(r + 1) * HEAD_DIM],
                          jnp.broadcast_to(slope_cols[r:r + 1, :], (Q_BLOCK, HEAD_DIM))], axis=1)
         for r in range(HEADS_PER_GROUP)], axis=0)
    t_row = q0 + (lax.broadcasted_iota(jnp.int32, (rows, LANES), 0) & (Q_BLOCK - 1))
    tq = q0 + lax.broadcasted_iota(jnp.int32, (Q_BLOCK, 1), 0)

    cmp_end = lax.broadcasted_iota(jnp.int32, (1, n_cmp), 1) * CMP_STRIDE + (CMP_BLOCK - 1)
    pc4, ov = _pair_attend(_pair_scores(q4, kc_ref[0, 0]), vc_ref[0, 0],
                           jnp.where(tq >= cmp_end, 0.0, NEG_BIG))
    o_cmp = jnp.where(t_row >= CMP_BLOCK - 1, ov[:, :LANES] / ov[:, LANES:], 0.0)

    w0 = pl.multiple_of(jnp.maximum(q0 - WINDOW, 0), Q_BLOCK)
    win_scores = _pair_scores(q4, kw_ref[0, 0, pl.ds(w0, WIN_KEYS), :])

    imp_l = lax.dot_general(mcs_t_ref[...], pc4, _NT, preferred_element_type=F32)
    imp = None
    for r in range(HEADS_PER_GROUP):
        cs = slice(r * Q_BLOCK, (r + 1) * Q_BLOCK)
        part = imp_l[:n_sel, cs] / imp_l[n_sel:n_sel + 1, cs]
        imp = part if imp is None else imp + part
    blk = lax.broadcasted_iota(jnp.int32, (n_sel, Q_BLOCK), 0)
    tq_l = q0 + lax.broadcasted_iota(jnp.int32, (n_sel, Q_BLOCK), 1)
    cur = tq_l >> 6
    forced = (blk == 0) | (blk == cur) | (blk == cur - 1)
    valid = blk * SEL_BLOCK <= tq_l
    score = jnp.where(forced, FORCED_SCORE, jnp.where(valid, imp, -1.0))
    rank = _block_ranks(score)
    chosen_t = jnp.where(valid, jnp.where(rank < float(min(SEL_TOPN, n_sel)), 1.0, 0.0), 0.0)
    chosen_ref[...] = chosen_t.T.astype(BF16)
    q4_ref[...] = q4

    n_chunks = expand_ref.shape[0]
    per_chunk = jnp.dot(chunk_of_ref[...], chosen_t.astype(BF16), preferred_element_type=F32)
    live = jnp.max(per_chunk, axis=1, keepdims=True) > 0.0
    weight = (1 << lax.broadcasted_iota(jnp.int32, (n_chunks, 1), 0)).astype(F32)
    live_bits = jnp.sum(jnp.where(live, weight, 0.0)).astype(jnp.int32)

    n_live = 0
    for c in range(n_chunks):
        slot_ref[n_live] = c
        n_live = n_live + ((live_bits >> c) & 1)

    vwin = vw_ref[0, 0, pl.ds(w0, WIN_KEYS), :]
    bias_w = win_bias_ref[jnp.minimum(qb, WINDOW // Q_BLOCK)]
    _, wv = _pair_attend(win_scores, vwin, bias_w)
    o_win = wv[:, :LANES] / wv[:, LANES:]

    for r in range(HEADS_PER_GROUP):
        for j in range(3):
            gate_b_ref[3 * r + j] = jnp.broadcast_to(gate_ref[0, :, 3 * r + j:3 * r + j + 1],
                                                     (Q_BLOCK, LANES))

    m_ref[...] = jnp.full(m_ref.shape, NEG_BIG, F32)

    def score_body(slot0, count):
        lane_max = [None] * HEADS_PER_GROUP
        starts, biases = [], []
        for u in range(count):
            c = slot_ref[slot0 + u]
            k0 = pl.multiple_of(c * SEL_CHUNK, SEL_CHUNK)
            hit = jnp.dot(chosen_ref[...], expand_ref[c], preferred_element_type=F32)
            pos = k0 + lax.broadcasted_iota(jnp.int32, (1, SEL_CHUNK), 1)
            biases.append(jnp.where(pos <= tq, (hit - 1.0) * (-NEG_BIG), NEG_BIG))
            starts.append(k0)
        for u in range(count):
            k = ks_ref[0, 0, pl.ds(starts[u], SEL_CHUNK), :]
            if count == 1:
                parts = _pair_scores(q4_ref[...], k)
            else:
                parts = [lax.dot_general(q4_ref[...], k, _NT, preferred_element_type=F32)]
            heads_per_part = HEADS_PER_GROUP // len(parts)
            for r in range(HEADS_PER_GROUP):
                lo = (r % heads_per_part) * Q_BLOCK
                s = parts[r // heads_per_part][lo:lo + Q_BLOCK] + biases[u]
                s_ref[slot0 + u, r * Q_BLOCK:(r + 1) * Q_BLOCK, :] = s
                mx = jnp.maximum(s[:, :LANES], s[:, LANES:])
                lane_max[r] = mx if lane_max[r] is None else jnp.maximum(lane_max[r], mx)
        for r in range(HEADS_PER_GROUP):
            rs = slice(r * Q_BLOCK, (r + 1) * Q_BLOCK)
            m_ref[rs, :] = jnp.maximum(m_ref[rs, :], lane_max[r])

    _for_each_group(n_live, score_body)
    m_ref[...] = jnp.broadcast_to(jnp.max(m_ref[...], axis=-1, keepdims=True), m_ref.shape)

    acc_ref[...] = jnp.zeros(acc_ref.shape, F32)

    def value_body(slot0, count):
        pv = None
        for u in range(count):
            slot = slot0 + u
            v0 = pl.multiple_of(slot_ref[slot] * SEL_CHUNK, SEL_CHUNK)
            v = vs_ref[0, 0, pl.ds(v0, SEL_CHUNK), :]
            p = jnp.concatenate([jnp.exp2(s_ref[slot, :, :LANES] - m_ref[...]),
                                 jnp.exp2(s_ref[slot, :, LANES:] - m_ref[...])],
                                axis=1).astype(BF16)
            if count == 1:
                half = rows // 2
                d = jnp.concatenate([jnp.dot(p[:half], v, preferred_element_type=F32),
                                     jnp.dot(p[half:], v, preferred_element_type=F32)], axis=0)
            else:
                d = jnp.dot(p, v, preferred_element_type=F32)
            pv = d if pv is None else pv + d
        acc_ref[...] += pv

    _for_each_group(n_live, value_body)
    o_sel = acc_ref[:, :LANES] / acc_ref[:, LANES:]

    lane = lax.broadcasted_iota(jnp.int32, (Q_BLOCK, LANES), 1)
    gated = []
    for r in range(HEADS_PER_GROUP):
        rs = slice(r * Q_BLOCK, (r + 1) * Q_BLOCK)
        gated.append(gate_b_ref[3 * r] * o_cmp[rs] + gate_b_ref[3 * r + 1] * o_sel[rs]
                     + gate_b_ref[3 * r + 2] * o_win[rs])
    o_ref[0] = jnp.concatenate(
        [jnp.where(lane < HEAD_DIM, gated[2 * i], gated[2 * i + 1])
         for i in range(HEADS_PER_GROUP // 2)], axis=-1)


def _attention(q, cmp_k, cmp_v, keys, values, gates):
    b, t, _ = q.shape
    n_cmp = cmp_k.shape[2]
    n_sel = t // SEL_BLOCK
    n_real_cmp = (t - CMP_BLOCK) // CMP_STRIDE + 1

    cs = np.arange(n_cmp)[None, :] * CMP_STRIDE
    js = np.arange(n_sel)[:, None] * SEL_BLOCK
    ov = np.minimum(cs + CMP_BLOCK, js + SEL_BLOCK) - np.maximum(cs, js)
    mcs_t = np.zeros((n_sel + SUBLANES, n_cmp), np.float32)
    mcs_t[:n_sel] = np.maximum(ov, 0).astype(np.float32) / CMP_BLOCK
    mcs_t[:n_sel, n_real_cmp:] = 0.0
    mcs_t[n_sel] = 1.0
    key_blk = (np.arange(t) // SEL_BLOCK).reshape(t // SEL_CHUNK, 1, SEL_CHUNK)
    expand = (key_blk == np.arange(n_sel)[None, :, None]).astype(np.float32)
    n_chunks = t // SEL_CHUNK
    chunk_of = (np.arange(n_sel)[None, :] // (SEL_CHUNK // SEL_BLOCK)
                == np.arange(n_chunks)[:, None]).astype(np.float32)
    rows = HEADS_PER_GROUP * Q_BLOCK
    dist = (np.arange(WINDOW // Q_BLOCK + 1)[:, None, None] * Q_BLOCK
            + np.arange(Q_BLOCK)[None, :, None] - np.arange(WIN_KEYS)[None, None, :])
    win_bias = np.where((dist >= 0) & (dist < WINDOW), 0.0, NEG_BIG).astype(np.float32)

    slope_l2 = LOG2E * np.power(2.0, -8.0 * np.arange(1, N_Q_HEADS + 1) / N_Q_HEADS)
    hi = slope_l2.astype(BF16).astype(np.float64)
    lo = (slope_l2 - hi).astype(BF16).astype(np.float64)
    cols = np.zeros((N_KV_GROUPS, SUBLANES, HEAD_DIM), np.float32)
    heads = cols[:, :HEADS_PER_GROUP].reshape(N_Q_HEADS, HEAD_DIM)
    heads[:, 0], heads[:, 1], heads[:, 2], heads[:, 3] = SEL_BLOCK * hi, SEL_BLOCK * lo, hi, lo
    cols[:, :HEADS_PER_GROUP] = heads.reshape(N_KV_GROUPS, HEADS_PER_GROUP, HEAD_DIM)

    def kv_spec(branch, rows, width):
        return pl.BlockSpec((1, 1, rows, width),
                            lambda bi, gi, qi, branch=branch: (bi, branch * N_KV_GROUPS + gi, 0, 0))

    return pl.pallas_call(
        _attn_kernel,
        grid=(b, N_KV_GROUPS, t // Q_BLOCK),
        in_specs=[
            pl.BlockSpec((1, Q_BLOCK, GROUP_WIDTH), lambda bi, gi, qi: (bi, qi, gi)),
            pl.BlockSpec((1, SUBLANES, HEAD_DIM), lambda bi, gi, qi: (gi, 0, 0)),
            kv_spec(0, n_cmp, AUG_DIM), kv_spec(0, n_cmp, WIDE_DIM),
            kv_spec(0, t, AUG_DIM), kv_spec(0, t, WIDE_DIM),
            kv_spec(1, t, AUG_DIM), kv_spec(1, t, WIDE_DIM),
            pl.BlockSpec((1, Q_BLOCK, LANES), lambda bi, gi, qi: (bi, qi, gi)),
            _const_spec(mcs_t.shape), _const_spec(expand.shape), _const_spec(chunk_of.shape),
            _const_spec(win_bias.shape),
        ],
        out_specs=pl.BlockSpec((1, Q_BLOCK, GROUP_WIDTH), lambda bi, gi, qi: (bi, qi, gi)),
        out_shape=jax.ShapeDtypeStruct((b, t, ATTN_WIDTH), F32),
        scratch_shapes=[
            pltpu.VMEM((rows, AUG_DIM), BF16),
            pltpu.VMEM((Q_BLOCK, n_sel), BF16),
            pltpu.VMEM((n_chunks, rows, SEL_CHUNK), F32),
            pltpu.VMEM((rows, LANES), F32),
            pltpu.VMEM((rows, WIDE_DIM), F32),
            pltpu.VMEM((3 * HEADS_PER_GROUP, Q_BLOCK, LANES), F32),
            pltpu.SMEM((n_chunks + 1,), jnp.int32),
        ],
        compiler_params=_params("parallel", "parallel", "arbitrary"),
        name="nsa_attn",
    )(q, jnp.asarray(cols, BF16), cmp_k, cmp_v, keys, values, keys, values, gates,
      jnp.asarray(mcs_t, BF16), jnp.asarray(expand, BF16), jnp.asarray(chunk_of, BF16),
      jnp.asarray(win_bias))


def _lru_kernel(xr_ref, xg_ref, cw_ref, cb_ref, wa_ref, ba_ref, wx_ref, bx_ref, lam_ref,
                g_ref, o_ref, xs_ref, a_ref, b_ref, h_ref):
    ti = pl.program_id(1)
    tt = xr_ref.shape[1]

    @pl.when(ti == 0)
    def _():
        xs_ref[...] = jnp.zeros_like(xs_ref)
        h_ref[...] = jnp.zeros_like(h_ref)

    n_groups = tt // SUBLANES
    as_groups = lambda v: v.reshape(n_groups, SUBLANES, LRU_WIDTH)
    sub = lax.broadcasted_iota(jnp.int32, (n_groups, SUBLANES, LRU_WIDTH), 1)

    x = as_groups(xr_ref[0])
    last = xs_ref[...]
    xc = cb_ref[...] + cw_ref[CONV_WIDTH - 1:CONV_WIDTH, :] * x
    for j in range(CONV_WIDTH - 1):
        lag = CONV_WIDTH - 1 - j
        rot = pltpu.roll(x, lag, axis=1)
        rot_before = jnp.concatenate([pltpu.roll(last, lag, axis=0)[None], rot[:-1]], axis=0)
        xc = xc + cw_ref[j:j + 1, :] * jnp.where(sub < lag, rot_before, rot)
    xs_ref[...] = x[n_groups - 1]
    xc = xc.reshape(tt, LRU_WIDTH)

    xb = xc.astype(BF16)
    r = jax.nn.sigmoid(jnp.dot(xb, wa_ref[...], preferred_element_type=F32) + ba_ref[...])
    i = jax.nn.sigmoid(jnp.dot(xb, wx_ref[...], preferred_element_type=F32) + bx_ref[...])
    neg_lam = -lam_ref[...]
    softplus = jnp.maximum(neg_lam, 0.0) + jnp.log1p(jnp.exp(-jnp.abs(neg_lam)))
    log_a = -LRU_C * r * softplus
    a = jnp.exp(log_a)
    th = jnp.tanh(log_a)
    mult = jnp.sqrt(jnp.maximum(-2.0 * th / (1.0 - th), 0.0))
    b = mult * i * xc

    a, b = as_groups(a), as_groups(b)
    for s in (1, 2, 4):
        ok = sub >= s
        a_prev = jnp.where(ok, pltpu.roll(a, s, axis=1), 1.0)
        b_prev = jnp.where(ok, pltpu.roll(b, s, axis=1), 0.0)
        b = a * b_prev + b
        a = a * a_prev
    a_ref[...] = a.reshape(tt, LRU_WIDTH)
    b_ref[...] = b.reshape(tt, LRU_WIDTH)

    def group_body(k, h):
        r0 = pl.multiple_of(k * SUBLANES, SUBLANES)
        h8 = a_ref[pl.ds(r0, SUBLANES), :] * h + b_ref[pl.ds(r0, SUBLANES), :]
        b_ref[pl.ds(r0, SUBLANES), :] = h8
        return jnp.broadcast_to(h8[SUBLANES - 1:SUBLANES, :], (SUBLANES, LRU_WIDTH))

    h_ref[...] = lax.fori_loop(0, tt // SUBLANES, group_body, h_ref[...])
    out = b_ref[...] * jax.nn.gelu(xg_ref[0])
    o_ref[0] = _rms(out, g_ref[...]).astype(BF16)


def _lru(xr, xg, conv_w, conv_b, wa, ba, wx, bx, lam, g):
    b, t, c = xr.shape
    tt = LRU_TOKENS
    tok = pl.BlockSpec((1, tt, c), lambda bi, ti: (bi, ti, 0))
    vec = _const_spec((1, c))
    return pl.pallas_call(
        _lru_kernel,
        grid=(b, t // tt),
        in_specs=[tok, tok, _const_spec((CONV_WIDTH, c)), vec, _const_spec((c, c)), vec,
                  _const_spec((c, c)), vec, vec, vec],
        out_specs=tok,
        out_shape=jax.ShapeDtypeStruct((b, t, c), BF16),
        scratch_shapes=[pltpu.VMEM((SUBLANES, c), F32), pltpu.VMEM((tt, c), F32),
                        pltpu.VMEM((tt, c), F32), pltpu.VMEM((SUBLANES, c), F32)],
        compiler_params=_params("parallel", "arbitrary"),
        name="rg_lru",
    )(xr, xg, conv_w, conv_b, wa, ba, wx, bx, lam, g)


def _mix_ffn_kernel(h_ref, attn_ref, lru_ref, ga_ref, mix_g_ref, wo_a_ref, wo_l_ref,
                    pre_g_ref, post_g_ref, wg_ref, wu_ref, wd_ref, o_ref, act_ref):
    ya = _rms(attn_ref[...], ga_ref[...]).astype(BF16)
    m = (jnp.dot(ya, wo_a_ref[...], preferred_element_type=F32)
         + jnp.dot(lru_ref[...], wo_l_ref[...], preferred_element_type=F32))
    h = h_ref[...] + _rms(m, mix_g_ref[...])
    o_ref[...] = _ffn_half_step(h, pre_g_ref, post_g_ref, wg_ref, wu_ref, wd_ref, act_ref)


def _mix_ffn(h, attn, lru, attn_g, mix_g, wo_a, wo_l, ffn_weights):
    n = h.shape[0]
    tm = FFN_TOKENS

    def tok(width):
        return pl.BlockSpec((tm, width), lambda i: (i, 0))

    return pl.pallas_call(
        _mix_ffn_kernel,
        grid=(n // tm,),
        in_specs=[tok(D_MODEL), tok(ATTN_WIDTH), tok(LRU_WIDTH), _const_spec((1, ATTN_WIDTH)),
                  _const_spec((1, D_MODEL)), _const_spec((ATTN_WIDTH, D_MODEL)),
                  _const_spec((LRU_WIDTH, D_MODEL))] + _ffn_weight_specs(),
        out_specs=tok(D_MODEL),
        out_shape=jax.ShapeDtypeStruct((n, D_MODEL), F32),
        scratch_shapes=[pltpu.VMEM((tm, D_FF), BF16)],
        compiler_params=_params("parallel"),
        name="mix_ffn",
    )(h, attn, lru, attn_g, mix_g, wo_a, wo_l, *ffn_weights)


def _pack_w_in(w_in):
    kv_cols = N_KV_GROUPS * HEAD_DIM
    gate_lo = ATTN_WIDTH + 6 * kv_cols
    gate_hi = gate_lo + 3 * N_Q_HEADS
    per_group = 3 * HEADS_PER_GROUP
    pad = jnp.zeros((w_in.shape[0], LANES - per_group), w_in.dtype)
    gate_slabs = []
    for gi in range(N_KV_GROUPS):
        gate_slabs += [w_in[:, gate_lo + gi * per_group:gate_lo + (gi + 1) * per_group], pad]
    return jnp.concatenate([w_in[:, :gate_lo]] + gate_slabs + [w_in[:, gate_hi:]], axis=1).astype(BF16)


def _block_diag(w):
    nb, d, e = w.shape
    eye = jnp.eye(nb, dtype=w.dtype)
    return jnp.einsum("nde,nm->ndme", w, eye).reshape(nb * d, nb * e).astype(BF16)


def _layer(h, p):
    b, t, d = h.shape
    n = b * t
    row = lambda v: v.reshape(1, -1)

    def ffn_weights(i):
        return (row(p[f"ffn{i}_pre_g"]), row(p[f"ffn{i}_post_g"]), p[f"ffn{i}_w_gate"].astype(BF16),
                p[f"ffn{i}_w_up"].astype(BF16), p[f"ffn{i}_w_down"].astype(BF16))

    h1, q, cmp_in, keys, values, gates, xr, xg = _ffn_proj(
        h, ffn_weights(1), row(p["mix_pre_g"]), _pack_w_in(p["w_in"]))
    cmp_k, cmp_v = _compress(cmp_in, p["cmp_k_pe"], p["cmp_k_w1"].astype(BF16),
                             p["cmp_k_w2"].astype(BF16), p["cmp_v_pe"],
                             p["cmp_v_w1"].astype(BF16), p["cmp_v_w2"].astype(BF16))
    attn = _attention(q, cmp_k, cmp_v, keys, values, gates)
    lru = _lru(xr, xg, p["conv_w"], row(p["conv_b"]), _block_diag(p["lru_w_a"]),
               row(p["lru_b_a"]), _block_diag(p["lru_w_x"]), row(p["lru_b_x"]),
               row(p["lru_lambda"]), row(p["lru_out_g"]))
    w_out = p["w_out"].astype(BF16)
    h3 = _mix_ffn(h1.reshape(n, d), attn.reshape(n, ATTN_WIDTH), lru.reshape(n, LRU_WIDTH),
                  row(p["attn_out_g"]), row(p["mix_post_g"]),
                  w_out[:ATTN_WIDTH], w_out[ATTN_WIDTH:], ffn_weights(2))
    return h3.reshape(b, t, d)


_PARAM_NAMES = (
    "ffn1_pre_g", "ffn1_post_g", "ffn1_w_gate", "ffn1_w_up", "ffn1_w_down",
    "mix_pre_g", "mix_post_g", "w_in", "cmp_k_pe", "cmp_k_w1", "cmp_k_w2",
    "cmp_v_pe", "cmp_v_w1", "cmp_v_w2", "conv_w", "conv_b", "lru_w_a", "lru_b_a",
    "lru_w_x", "lru_b_x", "lru_lambda", "attn_out_g", "lru_out_g", "w_out",
    "ffn2_pre_g", "ffn2_post_g", "ffn2_w_gate", "ffn2_w_up", "ffn2_w_down",
)


def kernel(x, ffn1_pre_g, ffn1_post_g, ffn1_w_gate, ffn1_w_up, ffn1_w_down, mix_pre_g, mix_post_g, w_in, cmp_k_pe, cmp_k_w1, cmp_k_w2, cmp_v_pe, cmp_v_w1, cmp_v_w2, conv_w, conv_b, lru_w_a, lru_b_a, lru_w_x, lru_b_x, lru_lambda, attn_out_g, lru_out_g, w_out, ffn2_pre_g, ffn2_post_g, ffn2_w_gate, ffn2_w_up, ffn2_w_down):
    stacked = dict(zip(_PARAM_NAMES, (
        ffn1_pre_g, ffn1_post_g, ffn1_w_gate, ffn1_w_up, ffn1_w_down, mix_pre_g, mix_post_g,
        w_in, cmp_k_pe, cmp_k_w1, cmp_k_w2, cmp_v_pe, cmp_v_w1, cmp_v_w2, conv_w, conv_b,
        lru_w_a, lru_b_a, lru_w_x, lru_b_x, lru_lambda, attn_out_g, lru_out_g, w_out,
        ffn2_pre_g, ffn2_post_g, ffn2_w_gate, ffn2_w_up, ffn2_w_down)))
    h = x
    for layer in range(ffn1_pre_g.shape[0]):
        h = _layer(h, {k: v[layer] for k, v in stacked.items()})
    return h
```

```python
import functools

import numpy as np
import jax
import jax.numpy as jnp
from jax import lax
from jax.experimental import pallas as pl
from jax.experimental.pallas import tpu as pltpu

F32 = jnp.float32
BF16 = jnp.bfloat16

D_MODEL = 1024
N_Q_HEADS = 8
HEAD_DIM = 64
N_KV_GROUPS = 2
HEADS_PER_GROUP = N_Q_HEADS // N_KV_GROUPS
ATTN_WIDTH = N_Q_HEADS * HEAD_DIM
GROUP_WIDTH = HEADS_PER_GROUP * HEAD_DIM
CMP_BLOCK = 32
CMP_STRIDE = 16
CMP_HIDDEN = 256
SEL_BLOCK = 64
SEL_TOPN = 16
WINDOW = 512
Q_BLOCK = 256
LRU_WIDTH = 512
LRU_BLOCKS = 8
CONV_WIDTH = 4
LRU_C = 8.0
D_FF = 2816
NORM_EPS = 1e-6

LANES = 128
SUBLANES = 8
VMEM_LIMIT_BYTES = 56 * 1024 * 1024

NEG_BIG = -1e30
FORCED_SCORE = 3e38
LOG2E = 1.4426950408889634
AUG_DIM = 2 * HEAD_DIM
WIDE_DIM = 4 * HEAD_DIM

FFN_TOKENS = 512
FFN_CHUNK = 256
LRU_PIECES = 32
SEL_CHUNK = 256
WIN_KEYS = WINDOW + Q_BLOCK

COL_Q = 0
COL_CMP = COL_Q + ATTN_WIDTH
COL_KV = COL_CMP + 2 * N_KV_GROUPS * HEAD_DIM
COL_GATE = COL_KV + 4 * N_KV_GROUPS * HEAD_DIM
COL_XR = COL_GATE + N_KV_GROUPS * LANES
COL_XG = COL_XR + LRU_WIDTH
PROJ_WIDTH = COL_XG + LRU_WIDTH


def _rms(x, g):
    ms = jnp.mean(x * x, axis=-1, keepdims=True)
    return x * lax.rsqrt(ms + NORM_EPS) * g


def _const_spec(shape):
    nd = len(shape)
    return pl.BlockSpec(shape, lambda *_: (0,) * nd, pipeline_mode=pl.Buffered(1))


def _params(*sem):
    return pltpu.CompilerParams(dimension_semantics=sem, vmem_limit_bytes=VMEM_LIMIT_BYTES)


def _ffn_half_step(x, pre_g_ref, post_g_ref, wg_ref, wu_ref, wd_ref, act_ref, side_work=()):
    xb = _rms(x, pre_g_ref[...]).astype(BF16)
    for c in range(D_FF // FFN_CHUNK):
        sl = slice(c * FFN_CHUNK, (c + 1) * FFN_CHUNK)
        gate = jnp.dot(xb, wg_ref[:, sl], preferred_element_type=F32)
        up = jnp.dot(xb, wu_ref[:, sl], preferred_element_type=F32)
        per_chunk = -(-len(side_work) // (D_FF // FFN_CHUNK))
        for work in side_work[c * per_chunk:(c + 1) * per_chunk]:
            work(gate)
        act_ref[:, sl] = (jax.nn.silu(gate) * up).astype(BF16)
    f = jnp.dot(act_ref[...], wd_ref[...], preferred_element_type=F32)
    return x + 0.5 * _rms(f, post_g_ref[...])


def _ffn_weight_specs():
    return [_const_spec((1, D_MODEL)), _const_spec((1, D_MODEL)), _const_spec((D_MODEL, D_FF)),
            _const_spec((D_MODEL, D_FF)), _const_spec((D_FF, D_MODEL))]


def _key_tail(pos, rows):
    lane = lax.broadcasted_iota(jnp.int32, (rows, HEAD_DIM), 1)
    hi = (pos >> 6).astype(F32)
    lo = (pos & (SEL_BLOCK - 1)).astype(F32)
    return jnp.where(lane < 2, hi, jnp.where(lane < 4, lo, 0.0))


def _wide_value(v):
    return jnp.concatenate([v, v, jnp.ones((v.shape[0], LANES), F32)], axis=1).astype(BF16)


def _ffn_proj_kernel(x_ref, pre_g_ref, post_g_ref, wg_ref, wu_ref, wd_ref, g_ref, w_ref,
                     h_ref, q_ref, cmp_ref, k_ref, v_ref, gate_ref, xr_ref, xg_ref, act_ref):
    tm = x_ref.shape[1]
    h = _ffn_half_step(x_ref[0], pre_g_ref, post_g_ref, wg_ref, wu_ref, wd_ref, act_ref)
    h_ref[0] = h
    hb = _rms(h, g_ref[...]).astype(BF16)
    p = jnp.dot(hb, w_ref[...], preferred_element_type=F32)
    q_ref[0] = (p[:, COL_Q:COL_CMP] * (HEAD_DIM ** -0.5 * LOG2E)).astype(BF16)
    cmp_ref[0] = p[:, COL_CMP:COL_KV]
    pos = pl.program_id(1) * tm + lax.broadcasted_iota(jnp.int32, (tm, 1), 0)
    key_tail = _key_tail(pos, tm)
    for i in range(4 * N_KV_GROUPS):
        lo = COL_KV + i * HEAD_DIM
        x = p[:, lo:lo + HEAD_DIM]
        branch, is_value, gi = i // (2 * N_KV_GROUPS), (i // N_KV_GROUPS) % 2, i % N_KV_GROUPS
        if is_value:
            v_ref[0, branch * N_KV_GROUPS + gi] = _wide_value(x)
        else:
            k_ref[0, branch * N_KV_GROUPS + gi] = jnp.concatenate([x, key_tail], axis=1).astype(BF16)
    gate_ref[0] = jax.nn.sigmoid(p[:, COL_GATE:COL_XR])
    xr_ref[0] = p[:, COL_XR:COL_XG]
    xg_ref[0] = p[:, COL_XG:PROJ_WIDTH]


def _ffn_proj(x, ffn_weights, g, w_packed):
    b, t, _ = x.shape
    tm = FFN_TOKENS

    def tok(width):
        return pl.BlockSpec((1, tm, width), lambda bi, ti: (bi, ti, 0))

    return pl.pallas_call(
        _ffn_proj_kernel,
        grid=(b, t // tm),
        in_specs=[tok(D_MODEL)] + _ffn_weight_specs()
        + [_const_spec((1, D_MODEL)), _const_spec((D_MODEL, PROJ_WIDTH))],
        out_specs=[
            tok(D_MODEL),
            tok(ATTN_WIDTH),
            tok(2 * N_KV_GROUPS * HEAD_DIM),
            pl.BlockSpec((1, 2 * N_KV_GROUPS, tm, AUG_DIM), lambda bi, ti: (bi, 0, ti, 0)),
            pl.BlockSpec((1, 2 * N_KV_GROUPS, tm, WIDE_DIM), lambda bi, ti: (bi, 0, ti, 0)),
            tok(N_KV_GROUPS * LANES),
            tok(LRU_WIDTH),
            tok(LRU_WIDTH),
        ],
        out_shape=[
            jax.ShapeDtypeStruct((b, t, D_MODEL), F32),
            jax.ShapeDtypeStruct((b, t, ATTN_WIDTH), BF16),
            jax.ShapeDtypeStruct((b, t, 2 * N_KV_GROUPS * HEAD_DIM), F32),
            jax.ShapeDtypeStruct((b, 2 * N_KV_GROUPS, t, AUG_DIM), BF16),
            jax.ShapeDtypeStruct((b, 2 * N_KV_GROUPS, t, WIDE_DIM), BF16),
            jax.ShapeDtypeStruct((b, t, N_KV_GROUPS * LANES), F32),
            jax.ShapeDtypeStruct((b, t, LRU_WIDTH), F32),
            jax.ShapeDtypeStruct((b, t, LRU_WIDTH), F32),
        ],
        scratch_shapes=[pltpu.VMEM((tm, D_FF), BF16)],
        compiler_params=_params("parallel", "parallel"),
        name="ffn_proj",
    )(x, *ffn_weights, g, w_packed)


def _compress_kernel(xk_ref, xv_ref, kpe_ref, kw1_ref, kw2_ref, vpe_ref, vw1_ref, vw2_ref,
                     ok_ref, ov_ref):
    n_chunks = xk_ref.shape[1] // CMP_STRIDE
    half = CMP_BLOCK // 2
    kinds = ((xk_ref, kpe_ref, kw1_ref, kw2_ref), (xv_ref, vpe_ref, vw1_ref, vw2_ref))
    top = [jnp.zeros((n_chunks, CMP_HIDDEN), F32) for _ in range(4)]
    bot = [jnp.zeros((n_chunks, CMP_HIDDEN), F32) for _ in range(4)]
    for l in range(half):
        rows = [ref[0, pl.ds(l, n_chunks, stride=CMP_STRIDE), :] for ref in (xk_ref, xv_ref)]
        for s in range(4):
            _, pe_ref, w1_ref, _ = kinds[s // N_KV_GROUPS]
            gi = s % N_KV_GROUPS
            xs = rows[s // N_KV_GROUPS][:, gi * HEAD_DIM:(gi + 1) * HEAD_DIM]
            x_top = (xs + pe_ref[l:l + 1, :]).astype(BF16)
            x_bot = (xs + pe_ref[half + l:half + l + 1, :]).astype(BF16)
            top[s] += jnp.dot(x_top, w1_ref[l * HEAD_DIM:(l + 1) * HEAD_DIM, :],
                              preferred_element_type=F32)
            bot[s] += jnp.dot(x_bot, w1_ref[(half + l) * HEAD_DIM:(half + l + 1) * HEAD_DIM, :],
                              preferred_element_type=F32)
    row = lax.broadcasted_iota(jnp.int32, (n_chunks, HEAD_DIM), 0)
    cmp_end = lax.broadcasted_iota(jnp.int32, (n_chunks, 1), 0) * CMP_STRIDE + (CMP_BLOCK - 1)
    key_tail = _key_tail(cmp_end, n_chunks)
    for s in range(4):
        w2_ref = kinds[s // N_KV_GROUPS][3]
        hidden = top[s] + pltpu.roll(bot[s], n_chunks - 1, axis=0)
        out = jnp.dot(jax.nn.gelu(hidden).astype(BF16), w2_ref[...], preferred_element_type=F32)
        out = jnp.where(row < n_chunks - 1, out, 0.0)
        if s // N_KV_GROUPS == 0:
            ok_ref[0, s % N_KV_GROUPS] = jnp.concatenate([out, key_tail], axis=1).astype(BF16)
        else:
            ov_ref[0, s % N_KV_GROUPS] = _wide_value(out)


def _compress(cmp_in, k_pe, k_w1, k_w2, v_pe, v_w1, v_w2):
    b, t, _ = cmp_in.shape
    n_chunks = t // CMP_STRIDE
    kv_cols = N_KV_GROUPS * HEAD_DIM
    return pl.pallas_call(
        _compress_kernel,
        grid=(b,),
        in_specs=[pl.BlockSpec((1, t, kv_cols), lambda bi: (bi, 0, 0)),
                  pl.BlockSpec((1, t, kv_cols), lambda bi: (bi, 0, 1)),
                  _const_spec(k_pe.shape), _const_spec(k_w1.shape), _const_spec(k_w2.shape),
                  _const_spec(v_pe.shape), _const_spec(v_w1.shape), _const_spec(v_w2.shape)],
        out_specs=[pl.BlockSpec((1, N_KV_GROUPS, n_chunks, AUG_DIM), lambda bi: (bi, 0, 0, 0)),
                   pl.BlockSpec((1, N_KV_GROUPS, n_chunks, WIDE_DIM), lambda bi: (bi, 0, 0, 0))],
        out_shape=[jax.ShapeDtypeStruct((b, N_KV_GROUPS, n_chunks, AUG_DIM), BF16),
                   jax.ShapeDtypeStruct((b, N_KV_GROUPS, n_chunks, WIDE_DIM), BF16)],
        compiler_params=_params("parallel"),
        name="compress",
    )(cmp_in, cmp_in, k_pe, k_w1, k_w2, v_pe, v_w1, v_w2)


_NT = (((1,), (1,)), ((), ()))


def _softmax_numerators(s, bias):
    n_slabs = s.shape[1] // LANES
    probs = []
    for r in range(s.shape[0] // Q_BLOCK):
        sb = s[r * Q_BLOCK:(r + 1) * Q_BLOCK] + bias
        slabs = [sb[:, j * LANES:(j + 1) * LANES] for j in range(n_slabs)]
        lane_max = functools.reduce(jnp.maximum, slabs)
        m = jnp.broadcast_to(jnp.max(lane_max, axis=-1, keepdims=True), (Q_BLOCK, LANES))
        probs.append(jnp.concatenate([jnp.exp2(x - m) for x in slabs], axis=1).astype(BF16))
    return jnp.concatenate(probs, axis=0)


def _pair_scores(q4, k):
    half = q4.shape[0] // 2
    return [lax.dot_general(q4[i * half:(i + 1) * half], k, _NT, preferred_element_type=F32)
            for i in range(2)]


def _pair_attend(scores, v, bias):
    probs = [_softmax_numerators(s, bias) for s in scores]
    outs = [jnp.dot(p, v, preferred_element_type=F32) for p in probs]
    return jnp.concatenate(probs, axis=0), jnp.concatenate(outs, axis=0)


def _for_each_group(n, body):
    def four(i, carry):
        body(4 * i, 2)
        body(4 * i + 2, 2)
        return carry

    lax.fori_loop(0, n // 4, four, 0)
    rest = (n // 4) * 4

    @pl.when(n - rest >= 2)
    def _():
        body(rest, 2)

    @pl.when((n - rest) % 2 == 1)
    def _():
        body(n - 1, 1)


def _block_ranks(score):
    n_blocks, width = score.shape
    n_slabs = n_blocks // SUBLANES
    slabs = [score[s * SUBLANES:(s + 1) * SUBLANES, :] for s in range(n_slabs)]
    ranks = [jnp.zeros((SUBLANES, width), F32) for _ in range(n_slabs)]
    sub = lax.broadcasted_iota(jnp.int32, (SUBLANES, width), 0)
    for j in range(n_blocks):
        other = jnp.broadcast_to(score[j:j + 1, :], (SUBLANES, width))
        for s in range(n_slabs):
            if s * SUBLANES > j:
                ahead = jnp.where(other >= slabs[s], 1.0, 0.0)
            elif (s + 1) * SUBLANES - 1 <= j:
                ahead = jnp.where(other > slabs[s], 1.0, 0.0)
            else:
                ahead = jnp.where(sub > j - s * SUBLANES, jnp.where(other >= slabs[s], 1.0, 0.0),
                                  jnp.where(other > slabs[s], 1.0, 0.0))
            ranks[s] = ranks[s] + ahead
    return jnp.concatenate(ranks, axis=0)


def _attn_kernel(q_ref, slope_ref, kc_ref, vc_ref, ks_ref, vs_ref, kw_ref, vw_ref, gate_ref,
                 mcs_t_ref, expand_ref, chunk_of_ref, win_bias_ref, o_ref,
                 q4_ref, chosen_ref, s_ref, m_ref, acc_ref, gate_b_ref, slot_ref):
    qb = pl.program_id(2)
    q0 = qb * Q_BLOCK
    rows = HEADS_PER_GROUP * Q_BLOCK
    n_cmp = kc_ref.shape[2]
    n_sel = expand_ref.shape[1]

    qblk = q_ref[0]
    slope_cols = slope_ref[0]
    q4 = jnp.concatenate(
        [jnp.concatenate([qblk[:, r * HEAD_DIM:(r + 1) * HEAD_DIM],
                          jnp.broadcast_to(slope_cols[r:r + 1, :], (Q_BLOCK, HEAD_DIM))], axis=1)
         for r in range(HEADS_PER_GROUP)], axis=0)
    t_row = q0 + (lax.broadcasted_iota(jnp.int32, (rows, LANES), 0) & (Q_BLOCK - 1))
    tq = q0 + lax.broadcasted_iota(jnp.int32, (Q_BLOCK, 1), 0)

    cmp_end = lax.broadcasted_iota(jnp.int32, (1, n_cmp), 1) * CMP_STRIDE + (CMP_BLOCK - 1)
    pc4, ov = _pair_attend(_pair_scores(q4, kc_ref[0, 0]), vc_ref[0, 0],
                           jnp.where(tq >= cmp_end, 0.0, NEG_BIG))
    o_cmp = jnp.where(t_row >= CMP_BLOCK - 1, ov[:, :LANES] / ov[:, LANES:], 0.0)

    w0 = pl.multiple_of(jnp.maximum(q0 - WINDOW, 0), Q_BLOCK)
    win_scores = _pair_scores(q4, kw_ref[0, 0, pl.ds(w0, WIN_KEYS), :])

    imp_l = lax.dot_general(mcs_t_ref[...], pc4, _NT, preferred_element_type=F32)
    imp = None
    for r in range(HEADS_PER_GROUP):
        cs = slice(r * Q_BLOCK, (r + 1) * Q_BLOCK)
        part = imp_l[:n_sel, cs] / imp_l[n_sel:n_sel + 1, cs]
        imp = part if imp is None else imp + part
    blk = lax.broadcasted_iota(jnp.int32, (n_sel, Q_BLOCK), 0)
    tq_l = q0 + lax.broadcasted_iota(jnp.int32, (n_sel, Q_BLOCK), 1)
    cur = tq_l >> 6
    forced = (blk == 0) | (blk == cur) | (blk == cur - 1)
    valid = blk * SEL_BLOCK <= tq_l
    score = jnp.where(forced, FORCED_SCORE, jnp.where(valid, imp, -1.0))
    rank = _block_ranks(score)
    chosen_t = jnp.where(valid, jnp.where(rank < float(min(SEL_TOPN, n_sel)), 1.0, 0.0), 0.0)
    chosen_ref[...] = chosen_t.T.astype(BF16)
    q4_ref[...] = q4

    n_chunks = expand_ref.shape[0]
    per_chunk = jnp.dot(chunk_of_ref[...], chosen_t.astype(BF16), preferred_element_type=F32)
    live = jnp.max(per_chunk, axis=1, keepdims=True) > 0.0
    weight = (1 << lax.broadcasted_iota(jnp.int32, (n_chunks, 1), 0)).astype(F32)
    live_bits = jnp.sum(jnp.where(live, weight, 0.0)).astype(jnp.int32)

    n_live = 0
    for c in range(n_chunks):
        slot_ref[n_live] = c
        n_live = n_live + ((live_bits >> c) & 1)

    vwin = vw_ref[0, 0, pl.ds(w0, WIN_KEYS), :]
    bias_w = win_bias_ref[jnp.minimum(qb, WINDOW // Q_BLOCK)]
    _, wv = _pair_attend(win_scores, vwin, bias_w)
    o_win = wv[:, :LANES] / wv[:, LANES:]

    for r in range(HEADS_PER_GROUP):
        for j in range(3):
            gate_b_ref[3 * r + j] = jnp.broadcast_to(gate_ref[0, :, 3 * r + j:3 * r + j + 1],
                                                     (Q_BLOCK, LANES))

    m_ref[...] = jnp.full(m_ref.shape, NEG_BIG, F32)

    def score_body(slot0, count):
        lane_max = [None] * HEADS_PER_GROUP
        starts, biases = [], []
        for u in range(count):
            c = slot_ref[slot0 + u]
            k0 = pl.multiple_of(c * SEL_CHUNK, SEL_CHUNK)
            hit = jnp.dot(chosen_ref[...], expand_ref[c], preferred_element_type=F32)
            pos = k0 + lax.broadcasted_iota(jnp.int32, (1, SEL_CHUNK), 1)
            biases.append(jnp.where(pos <= tq, (hit - 1.0) * (-NEG_BIG), NEG_BIG))
            starts.append(k0)
        for u in range(count):
            k = ks_ref[0, 0, pl.ds(starts[u], SEL_CHUNK), :]
            if count == 1:
                parts = _pair_scores(q4_ref[...], k)
            else:
                parts = [lax.dot_general(q4_ref[...], k, _NT, preferred_element_type=F32)]
            heads_per_part = HEADS_PER_GROUP // len(parts)
            for r in range(HEADS_PER_GROUP):
                lo = (r % heads_per_part) * Q_BLOCK
                s = parts[r // heads_per_part][lo:lo + Q_BLOCK] + biases[u]
                s_ref[slot0 + u, r * Q_BLOCK:(r + 1) * Q_BLOCK, :] = s
                mx = jnp.maximum(s[:, :LANES], s[:, LANES:])
                lane_max[r] = mx if lane_max[r] is None else jnp.maximum(lane_max[r], mx)
        for r in range(HEADS_PER_GROUP):
            rs = slice(r * Q_BLOCK, (r + 1) * Q_BLOCK)
            m_ref[rs, :] = jnp.maximum(m_ref[rs, :], lane_max[r])

    _for_each_group(n_live, score_body)
    m_ref[...] = jnp.broadcast_to(jnp.max(m_ref[...], axis=-1, keepdims=True), m_ref.shape)

    acc_ref[...] = jnp.zeros(acc_ref.shape, F32)

    def value_body(slot0, count):
        pv = None
        for u in range(count):
            slot = slot0 + u
            v0 = pl.multiple_of(slot_ref[slot] * SEL_CHUNK, SEL_CHUNK)
            v = vs_ref[0, 0, pl.ds(v0, SEL_CHUNK), :]
            p = jnp.concatenate([jnp.exp2(s_ref[slot, :, :LANES] - m_ref[...]),
                                 jnp.exp2(s_ref[slot, :, LANES:] - m_ref[...])],
                                axis=1).astype(BF16)
            if count == 1:
                half = rows // 2
                d = jnp.concatenate([jnp.dot(p[:half], v, preferred_element_type=F32),
                                     jnp.dot(p[half:], v, preferred_element_type=F32)], axis=0)
            else:
                d = jnp.dot(p, v, preferred_element_type=F32)
            pv = d if pv is None else pv + d
        acc_ref[...] += pv

    _for_each_group(n_live, value_body)
    o_sel = acc_ref[:, :LANES] / acc_ref[:, LANES:]

    lane = lax.broadcasted_iota(jnp.int32, (Q_BLOCK, LANES), 1)
    gated = []
    for r in range(HEADS_PER_GROUP):
        rs = slice(r * Q_BLOCK, (r + 1) * Q_BLOCK)
        gated.append(gate_b_ref[3 * r] * o_cmp[rs] + gate_b_ref[3 * r + 1] * o_sel[rs]
                     + gate_b_ref[3 * r + 2] * o_win[rs])
    o_ref[0] = jnp.concatenate(
        [jnp.where(lane < HEAD_DIM, gated[2 * i], gated[2 * i + 1])
         for i in range(HEADS_PER_GROUP // 2)], axis=-1)


def _attention(q, cmp_k, cmp_v, keys, values, gates):
    b, t, _ = q.shape
    n_cmp = cmp_k.shape[2]
    n_sel = t // SEL_BLOCK
    n_real_cmp = (t - CMP_BLOCK) // CMP_STRIDE + 1

    cs = np.arange(n_cmp)[None, :] * CMP_STRIDE
    js = np.arange(n_sel)[:, None] * SEL_BLOCK
    ov = np.minimum(cs + CMP_BLOCK, js + SEL_BLOCK) - np.maximum(cs, js)
    mcs_t = np.zeros((n_sel + SUBLANES, n_cmp), np.float32)
    mcs_t[:n_sel] = np.maximum(ov, 0).astype(np.float32) / CMP_BLOCK
    mcs_t[:n_sel, n_real_cmp:] = 0.0
    mcs_t[n_sel] = 1.0
    key_blk = (np.arange(t) // SEL_BLOCK).reshape(t // SEL_CHUNK, 1, SEL_CHUNK)
    expand = (key_blk == np.arange(n_sel)[None, :, None]).astype(np.float32)
    n_chunks = t // SEL_CHUNK
    chunk_of = (np.arange(n_sel)[None, :] // (SEL_CHUNK // SEL_BLOCK)
                == np.arange(n_chunks)[:, None]).astype(np.float32)
    rows = HEADS_PER_GROUP * Q_BLOCK
    dist = (np.arange(WINDOW // Q_BLOCK + 1)[:, None, None] * Q_BLOCK
            + np.arange(Q_BLOCK)[None, :, None] - np.arange(WIN_KEYS)[None, None, :])
    win_bias = np.where((dist >= 0) & (dist < WINDOW), 0.0, NEG_BIG).astype(np.float32)

    slope_l2 = LOG2E * np.power(2.0, -8.0 * np.arange(1, N_Q_HEADS + 1) / N_Q_HEADS)
    hi = slope_l2.astype(BF16).astype(np.float64)
    lo = (slope_l2 - hi).astype(BF16).astype(np.float64)
    cols = np.zeros((N_KV_GROUPS, SUBLANES, HEAD_DIM), np.float32)
    heads = cols[:, :HEADS_PER_GROUP].reshape(N_Q_HEADS, HEAD_DIM)
    heads[:, 0], heads[:, 1], heads[:, 2], heads[:, 3] = SEL_BLOCK * hi, SEL_BLOCK * lo, hi, lo
    cols[:, :HEADS_PER_GROUP] = heads.reshape(N_KV_GROUPS, HEADS_PER_GROUP, HEAD_DIM)

    def kv_spec(branch, rows, width):
        return pl.BlockSpec((1, 1, rows, width),
                            lambda bi, gi, qi, branch=branch: (bi, branch * N_KV_GROUPS + gi, 0, 0))

    return pl.pallas_call(
        _attn_kernel,
        grid=(b, N_KV_GROUPS, t // Q_BLOCK),
        in_specs=[
            pl.BlockSpec((1, Q_BLOCK, GROUP_WIDTH), lambda bi, gi, qi: (bi, qi, gi)),
            pl.BlockSpec((1, SUBLANES, HEAD_DIM), lambda bi, gi, qi: (gi, 0, 0)),
            kv_spec(0, n_cmp, AUG_DIM), kv_spec(0, n_cmp, WIDE_DIM),
            kv_spec(0, t, AUG_DIM), kv_spec(0, t, WIDE_DIM),
            kv_spec(1, t, AUG_DIM), kv_spec(1, t, WIDE_DIM),
            pl.BlockSpec((1, Q_BLOCK, LANES), lambda bi, gi, qi: (bi, qi, gi)),
            _const_spec(mcs_t.shape), _const_spec(expand.shape), _const_spec(chunk_of.shape),
            _const_spec(win_bias.shape),
        ],
        out_specs=pl.BlockSpec((1, Q_BLOCK, GROUP_WIDTH), lambda bi, gi, qi: (bi, qi, gi)),
        out_shape=jax.ShapeDtypeStruct((b, t, ATTN_WIDTH), F32),
        scratch_shapes=[
            pltpu.VMEM((rows, AUG_DIM), BF16),
            pltpu.VMEM((Q_BLOCK, n_sel), BF16),
            pltpu.VMEM((n_chunks, rows, SEL_CHUNK), F32),
            pltpu.VMEM((rows, LANES), F32),
            pltpu.VMEM((rows, WIDE_DIM), F32),
            pltpu.VMEM((3 * HEADS_PER_GROUP, Q_BLOCK, LANES), F32),
            pltpu.SMEM((n_chunks + 1,), jnp.int32),
        ],
        compiler_params=_params("parallel", "parallel", "arbitrary"),
        name="nsa_attn",
    )(q, jnp.asarray(cols, BF16), cmp_k, cmp_v, keys, values, keys, values, gates,
      jnp.asarray(mcs_t, BF16), jnp.asarray(expand, BF16), jnp.asarray(chunk_of, BF16),
      jnp.asarray(win_bias))


def _lru_conv_and_gates(xr, lru_refs, xs_ref, xc_ref, a_ref, b_ref):
    cw_ref, cb_ref, wa_ref, ba_ref, wx_ref, bx_ref, _, _ = lru_refs
    tt = xr.shape[0]

    n_groups = tt // SUBLANES
    as_groups = lambda v: v.reshape(n_groups, SUBLANES, LRU_WIDTH)
    sub = lax.broadcasted_iota(jnp.int32, (n_groups, SUBLANES, LRU_WIDTH), 1)

    x = as_groups(xr)
    last = xs_ref[...]
    xc = cb_ref[...] + cw_ref[CONV_WIDTH - 1:CONV_WIDTH, :] * x
    for j in range(CONV_WIDTH - 1):
        lag = CONV_WIDTH - 1 - j
        rot = pltpu.roll(x, lag, axis=1)
        rot_before = jnp.concatenate([pltpu.roll(last, lag, axis=0)[None], rot[:-1]], axis=0)
        xc = xc + cw_ref[j:j + 1, :] * jnp.where(sub < lag, rot_before, rot)
    xs_ref[...] = x[n_groups - 1]
    xc = xc.reshape(tt, LRU_WIDTH)
    xc_ref[...] = xc
    xb = xc.astype(BF16)
    a_ref[...] = jnp.dot(xb, wa_ref[...], preferred_element_type=F32) + ba_ref[...]
    b_ref[...] = jnp.dot(xb, wx_ref[...], preferred_element_type=F32) + bx_ref[...]


def _lru_local_scan(rows, lam_ref, xc_ref, a_ref, b_ref, anchor):
    n_rows = rows.stop - rows.start
    r = jax.nn.sigmoid(a_ref[rows, :])
    i = jax.nn.sigmoid(b_ref[rows, :])
    neg_lam = -lam_ref[...]
    softplus = jnp.maximum(neg_lam, 0.0) + jnp.log1p(jnp.exp(-jnp.abs(neg_lam)))
    half_word = jnp.uint32(16)
    zero_bits = lax.shift_right_logical(
        lax.shift_right_logical(pltpu.bitcast(anchor[0:1, 0:LANES], jnp.uint32), half_word),
        half_word)
    zero_bits = jnp.concatenate([zero_bits] * (LRU_WIDTH // LANES), axis=1)
    softplus = pltpu.bitcast(pltpu.bitcast(softplus, jnp.uint32) | zero_bits, F32)
    log_a = -LRU_C * r * softplus
    a = jnp.exp(log_a)
    th = jnp.tanh(log_a)
    mult = jnp.sqrt(jnp.maximum(-2.0 * th / (1.0 - th), 0.0))
    b = mult * i * xc_ref[rows, :]

    n_groups = n_rows // SUBLANES
    sub = lax.broadcasted_iota(jnp.int32, (n_groups, SUBLANES, LRU_WIDTH), 1)
    a = a.reshape(n_groups, SUBLANES, LRU_WIDTH)
    b = b.reshape(n_groups, SUBLANES, LRU_WIDTH)
    for s in (1, 2, 4):
        ok = sub >= s
        a_prev = jnp.where(ok, pltpu.roll(a, s, axis=1), 1.0)
        b_prev = jnp.where(ok, pltpu.roll(b, s, axis=1), 0.0)
        b = a * b_prev + b
        a = a * a_prev
    a_ref[rows, :] = a.reshape(n_rows, LRU_WIDTH)
    b_ref[rows, :] = b.reshape(n_rows, LRU_WIDTH)


def _lru_carry(a_ref, b_ref, h_ref):
    def group_body(k, h):
        r0 = pl.multiple_of(k * SUBLANES, SUBLANES)
        h8 = a_ref[pl.ds(r0, SUBLANES), :] * h + b_ref[pl.ds(r0, SUBLANES), :]
        b_ref[pl.ds(r0, SUBLANES), :] = h8
        return jnp.broadcast_to(h8[SUBLANES - 1:SUBLANES, :], (SUBLANES, LRU_WIDTH))

    h_ref[...] = lax.fori_loop(0, a_ref.shape[0] // SUBLANES, group_body, h_ref[...])


def _lru_specs():
    vec = _const_spec((1, LRU_WIDTH))
    mat = _const_spec((LRU_WIDTH, LRU_WIDTH))
    return [_const_spec((CONV_WIDTH, LRU_WIDTH)), vec, mat, vec, mat, vec, vec, vec]


def _mix_ffn_kernel(tiles_per_seq, h_ref, attn_ref, xr_ref, xg_ref, *refs):
    lru_refs = refs[:8]
    ga_ref, mix_g_ref, wo_a_ref, wo_l_ref = refs[8:12]
    ffn_refs = refs[12:17]
    o_ref, act_ref, xs_ref, xc_ref, a_ref, b_ref, hc_ref, y_ref = refs[17:]
    s = pl.program_id(0)
    tm = xr_ref.shape[0]

    @pl.when(s == 0)
    def _():
        y_ref[...] = jnp.zeros_like(y_ref)

    @pl.when(s % tiles_per_seq == 0)
    def _():
        xs_ref[...] = jnp.zeros_like(xs_ref)
        hc_ref[...] = jnp.zeros_like(hc_ref)

    _lru_conv_and_gates(xr_ref[...], lru_refs, xs_ref, xc_ref, a_ref, b_ref)
    rows_per_piece = tm // LRU_PIECES
    lru_pieces = [functools.partial(_lru_local_scan,
                                    slice(i * rows_per_piece, (i + 1) * rows_per_piece),
                                    lru_refs[6], xc_ref, a_ref, b_ref)
                  for i in range(LRU_PIECES)]

    ya = _rms(attn_ref[...], ga_ref[...]).astype(BF16)
    m = (jnp.dot(ya, wo_a_ref[...], preferred_element_type=F32)
         + jnp.dot(y_ref[...], wo_l_ref[...], preferred_element_type=F32))
    h = h_ref[...] + _rms(m, mix_g_ref[...])
    o_ref[...] = _ffn_half_step(h, *ffn_refs, act_ref, side_work=lru_pieces)

    _lru_carry(a_ref, b_ref, hc_ref)
    out = b_ref[...] * jax.nn.gelu(xg_ref[...])
    y_ref[...] = _rms(out, lru_refs[7][...]).astype(BF16)


def _mix_ffn(h, attn, xr, xg, tiles_per_seq, lru_params, attn_g, mix_g, wo_a, wo_l, ffn_weights):
    n = h.shape[0]
    tm = FFN_TOKENS
    n_tiles = n // tm

    def lagging(width):
        return pl.BlockSpec((tm, width), lambda i: (jnp.maximum(i - 1, 0), 0))

    def leading(width):
        return pl.BlockSpec((tm, width), lambda i: (jnp.minimum(i, n_tiles - 1), 0))

    return pl.pallas_call(
        functools.partial(_mix_ffn_kernel, tiles_per_seq),
        grid=(n_tiles + 1,),
        in_specs=[lagging(D_MODEL), lagging(ATTN_WIDTH), leading(LRU_WIDTH), leading(LRU_WIDTH)]
        + _lru_specs()
        + [_const_spec((1, ATTN_WIDTH)), _const_spec((1, D_MODEL)),
           _const_spec((ATTN_WIDTH, D_MODEL)), _const_spec((LRU_WIDTH, D_MODEL))]
        + _ffn_weight_specs(),
        out_specs=lagging(D_MODEL),
        out_shape=jax.ShapeDtypeStruct((n, D_MODEL), F32),
        scratch_shapes=[pltpu.VMEM((tm, D_FF), BF16),
                        pltpu.VMEM((SUBLANES, LRU_WIDTH), F32),
                        pltpu.VMEM((tm, LRU_WIDTH), F32),
                        pltpu.VMEM((tm, LRU_WIDTH), F32),
                        pltpu.VMEM((tm, LRU_WIDTH), F32),
                        pltpu.VMEM((SUBLANES, LRU_WIDTH), F32),
                        pltpu.VMEM((tm, LRU_WIDTH), BF16)],
        compiler_params=_params("arbitrary"),
        name="mix_ffn",
    )(h, attn, xr, xg, *lru_params, attn_g, mix_g, wo_a, wo_l, *ffn_weights)


def _pack_w_in(w_in):
    kv_cols = N_KV_GROUPS * HEAD_DIM
    gate_lo = ATTN_WIDTH + 6 * kv_cols
    gate_hi = gate_lo + 3 * N_Q_HEADS
    per_group = 3 * HEADS_PER_GROUP
    pad = jnp.zeros((w_in.shape[0], LANES - per_group), w_in.dtype)
    gate_slabs = []
    for gi in range(N_KV_GROUPS):
        gate_slabs += [w_in[:, gate_lo + gi * per_group:gate_lo + (gi + 1) * per_group], pad]
    return jnp.concatenate([w_in[:, :gate_lo]] + gate_slabs + [w_in[:, gate_hi:]], axis=1).astype(BF16)


def _block_diag(w):
    nb, d, e = w.shape
    eye = jnp.eye(nb, dtype=w.dtype)
    return jnp.einsum("nde,nm->ndme", w, eye).reshape(nb * d, nb * e).astype(BF16)


def _layer(h, p):
    b, t, d = h.shape
    n = b * t
    row = lambda v: v.reshape(1, -1)

    def ffn_weights(i):
        return (row(p[f"ffn{i}_pre_g"]), row(p[f"ffn{i}_post_g"]), p[f"ffn{i}_w_gate"].astype(BF16),
                p[f"ffn{i}_w_up"].astype(BF16), p[f"ffn{i}_w_down"].astype(BF16))

    h1, q, cmp_in, keys, values, gates, xr, xg = _ffn_proj(
        h, ffn_weights(1), row(p["mix_pre_g"]), _pack_w_in(p["w_in"]))
    cmp_k, cmp_v = _compress(cmp_in, p["cmp_k_pe"], p["cmp_k_w1"].astype(BF16),
                             p["cmp_k_w2"].astype(BF16), p["cmp_v_pe"],
                             p["cmp_v_w1"].astype(BF16), p["cmp_v_w2"].astype(BF16))
    attn = _attention(q, cmp_k, cmp_v, keys, values, gates)
    lru_params = (p["conv_w"], row(p["conv_b"]), _block_diag(p["lru_w_a"]), row(p["lru_b_a"]),
                  _block_diag(p["lru_w_x"]), row(p["lru_b_x"]), row(p["lru_lambda"]),
                  row(p["lru_out_g"]))
    w_out = p["w_out"].astype(BF16)
    h3 = _mix_ffn(h1.reshape(n, d), attn.reshape(n, ATTN_WIDTH), xr.reshape(n, LRU_WIDTH),
                  xg.reshape(n, LRU_WIDTH), t // FFN_TOKENS, lru_params,
                  row(p["attn_out_g"]), row(p["mix_post_g"]),
                  w_out[:ATTN_WIDTH], w_out[ATTN_WIDTH:], ffn_weights(2))
    return h3.reshape(b, t, d)


_PARAM_NAMES = (
    "ffn1_pre_g", "ffn1_post_g", "ffn1_w_gate", "ffn1_w_up", "ffn1_w_down",
    "mix_pre_g", "mix_post_g", "w_in", "cmp_k_pe", "cmp_k_w1", "cmp_k_w2",
    "cmp_v_pe", "cmp_v_w1", "cmp_v_w2", "conv_w", "conv_b", "lru_w_a", "lru_b_a",
    "lru_w_x", "lru_b_x", "lru_lambda", "attn_out_g", "lru_out_g", "w_out",
    "ffn2_pre_g", "ffn2_post_g", "ffn2_w_gate", "ffn2_w_up", "ffn2_w_down",
)


def kernel(x, ffn1_pre_g, ffn1_post_g, ffn1_w_gate, ffn1_w_up, ffn1_w_down, mix_pre_g, mix_post_g, w_in, cmp_k_pe, cmp_k_w1, cmp_k_w2, cmp_v_pe, cmp_v_w1, cmp_v_w2, conv_w, conv_b, lru_w_a, lru_b_a, lru_w_x, lru_b_x, lru_lambda, attn_out_g, lru_out_g, w_out, ffn2_pre_g, ffn2_post_g, ffn2_w_gate, ffn2_w_up, ffn2_w_down):
    stacked = dict(zip(_PARAM_NAMES, (
        ffn1_pre_g, ffn1_post_g, ffn1_w_gate, ffn1_w_up, ffn1_w_down, mix_pre_g, mix_post_g,
        w_in, cmp_k_pe, cmp_k_w1, cmp_k_w2, cmp_v_pe, cmp_v_w1, cmp_v_w2, conv_w, conv_b,
        lru_w_a, lru_b_a, lru_w_x, lru_b_x, lru_lambda, attn_out_g, lru_out_g, w_out,
        ffn2_pre_g, ffn2_post_g, ffn2_w_gate, ffn2_w_up, ffn2_w_down)))
    h = x
    for layer in range(ffn1_pre_g.shape[0]):
        h = _layer(h, {k: v[layer] for k, v in stacked.items()})
    return h
```

```python
import functools

import numpy as np
import jax
import jax.numpy as jnp
from jax import lax
from jax.experimental import pallas as pl
from jax.experimental.pallas import tpu as pltpu

F32 = jnp.float32
BF16 = jnp.bfloat16

D_MODEL = 1024
N_Q_HEADS = 8
HEAD_DIM = 64
N_KV_GROUPS = 2
HEADS_PER_GROUP = N_Q_HEADS // N_KV_GROUPS
ATTN_WIDTH = N_Q_HEADS * HEAD_DIM
GROUP_WIDTH = HEADS_PER_GROUP * HEAD_DIM
CMP_BLOCK = 32
CMP_STRIDE = 16
CMP_HIDDEN = 256
SEL_BLOCK = 64
SEL_TOPN = 16
WINDOW = 512
Q_BLOCK = 256
LRU_WIDTH = 512
LRU_BLOCKS = 8
CONV_WIDTH = 4
LRU_C = 8.0
D_FF = 2816
NORM_EPS = 1e-6

LANES = 128
SUBLANES = 8
VMEM_LIMIT_BYTES = 56 * 1024 * 1024

NEG_BIG = -1e30
FORCED_SCORE = 3e38
LOG2E = 1.4426950408889634
AUG_DIM = 2 * HEAD_DIM
WIDE_DIM = 4 * HEAD_DIM

FFN_TOKENS = 512
FFN_CHUNK = 256
LRU_TOKENS = 1024
SEL_CHUNK = 256
WIN_KEYS = WINDOW + Q_BLOCK

COL_Q = 0
COL_CMP = COL_Q + ATTN_WIDTH
COL_KV = COL_CMP + 2 * N_KV_GROUPS * HEAD_DIM
COL_GATE = COL_KV + 4 * N_KV_GROUPS * HEAD_DIM
COL_XR = COL_GATE + N_KV_GROUPS * LANES
COL_XG = COL_XR + LRU_WIDTH
PROJ_WIDTH = COL_XG + LRU_WIDTH


def _rms(x, g):
    ms = jnp.mean(x * x, axis=-1, keepdims=True)
    return x * lax.rsqrt(ms + NORM_EPS) * g


def _const_spec(shape):
    nd = len(shape)
    return pl.BlockSpec(shape, lambda *_: (0,) * nd, pipeline_mode=pl.Buffered(1))


def _params(*sem):
    return pltpu.CompilerParams(dimension_semantics=sem, vmem_limit_bytes=VMEM_LIMIT_BYTES)


def _ffn_half_step(x, pre_g_ref, post_g_ref, wg_ref, wu_ref, wd_ref, act_ref):
    xb = _rms(x, pre_g_ref[...]).astype(BF16)
    for c in range(D_FF // FFN_CHUNK):
        sl = slice(c * FFN_CHUNK, (c + 1) * FFN_CHUNK)
        gate = jnp.dot(xb, wg_ref[:, sl], preferred_element_type=F32)
        up = jnp.dot(xb, wu_ref[:, sl], preferred_element_type=F32)
        act_ref[:, sl] = (jax.nn.silu(gate) * up).astype(BF16)
    f = jnp.dot(act_ref[...], wd_ref[...], preferred_element_type=F32)
    return x + 0.5 * _rms(f, post_g_ref[...])


def _ffn_weight_specs():
    return [_const_spec((1, D_MODEL)), _const_spec((1, D_MODEL)), _const_spec((D_MODEL, D_FF)),
            _const_spec((D_MODEL, D_FF)), _const_spec((D_FF, D_MODEL))]


def _key_tail(pos, rows):
    lane = lax.broadcasted_iota(jnp.int32, (rows, HEAD_DIM), 1)
    hi = (pos >> 6).astype(F32)
    lo = (pos & (SEL_BLOCK - 1)).astype(F32)
    return jnp.where(lane < 2, hi, jnp.where(lane < 4, lo, 0.0))


def _wide_value(v):
    return jnp.concatenate([v, v, jnp.ones((v.shape[0], LANES), F32)], axis=1).astype(BF16)


def _ffn_proj_kernel(x_ref, pre_g_ref, post_g_ref, wg_ref, wu_ref, wd_ref, g_ref, w_ref,
                     h_ref, q_ref, cmp_ref, k_ref, v_ref, gate_ref, xr_ref, xg_ref, act_ref):
    tm = x_ref.shape[1]
    h = _ffn_half_step(x_ref[0], pre_g_ref, post_g_ref, wg_ref, wu_ref, wd_ref, act_ref)
    h_ref[0] = h
    hb = _rms(h, g_ref[...]).astype(BF16)
    p = jnp.dot(hb, w_ref[...], preferred_element_type=F32)
    q_ref[0] = (p[:, COL_Q:COL_CMP] * (HEAD_DIM ** -0.5 * LOG2E)).astype(BF16)
    cmp_ref[0] = p[:, COL_CMP:COL_KV]
    pos = pl.program_id(1) * tm + lax.broadcasted_iota(jnp.int32, (tm, 1), 0)
    key_tail = _key_tail(pos, tm)
    for i in range(4 * N_KV_GROUPS):
        lo = COL_KV + i * HEAD_DIM
        x = p[:, lo:lo + HEAD_DIM]
        branch, is_value, gi = i // (2 * N_KV_GROUPS), (i // N_KV_GROUPS) % 2, i % N_KV_GROUPS
        if is_value:
            v_ref[0, branch * N_KV_GROUPS + gi] = _wide_value(x)
        else:
            k_ref[0, branch * N_KV_GROUPS + gi] = jnp.concatenate([x, key_tail], axis=1).astype(BF16)
    gate_ref[0] = jax.nn.sigmoid(p[:, COL_GATE:COL_XR])
    xr_ref[0] = p[:, COL_XR:COL_XG]
    xg_ref[0] = p[:, COL_XG:PROJ_WIDTH]


def _ffn_proj(x, ffn_weights, g, w_packed):
    b, t, _ = x.shape
    tm = FFN_TOKENS

    def tok(width):
        return pl.BlockSpec((1, tm, width), lambda bi, ti: (bi, ti, 0))

    return pl.pallas_call(
        _ffn_proj_kernel,
        grid=(b, t // tm),
        in_specs=[tok(D_MODEL)] + _ffn_weight_specs()
        + [_const_spec((1, D_MODEL)), _const_spec((D_MODEL, PROJ_WIDTH))],
        out_specs=[
            tok(D_MODEL),
            tok(ATTN_WIDTH),
            tok(2 * N_KV_GROUPS * HEAD_DIM),
            pl.BlockSpec((1, 2 * N_KV_GROUPS, tm, AUG_DIM), lambda bi, ti: (bi, 0, ti, 0)),
            pl.BlockSpec((1, 2 * N_KV_GROUPS, tm, WIDE_DIM), lambda bi, ti: (bi, 0, ti, 0)),
            tok(N_KV_GROUPS * LANES),
            tok(LRU_WIDTH),
            tok(LRU_WIDTH),
        ],
        out_shape=[
            jax.ShapeDtypeStruct((b, t, D_MODEL), F32),
            jax.ShapeDtypeStruct((b, t, ATTN_WIDTH), BF16),
            jax.ShapeDtypeStruct((b, t, 2 * N_KV_GROUPS * HEAD_DIM), F32),
            jax.ShapeDtypeStruct((b, 2 * N_KV_GROUPS, t, AUG_DIM), BF16),
            jax.ShapeDtypeStruct((b, 2 * N_KV_GROUPS, t, WIDE_DIM), BF16),
            jax.ShapeDtypeStruct((b, t, N_KV_GROUPS * LANES), F32),
            jax.ShapeDtypeStruct((b, t, LRU_WIDTH), F32),
            jax.ShapeDtypeStruct((b, t, LRU_WIDTH), F32),
        ],
        scratch_shapes=[pltpu.VMEM((tm, D_FF), BF16)],
        compiler_params=_params("parallel", "parallel"),
        name="ffn_proj",
    )(x, *ffn_weights, g, w_packed)


def _compress_kernel(xk_ref, xv_ref, kpe_ref, kw1_ref, kw2_ref, vpe_ref, vw1_ref, vw2_ref,
                     ok_ref, ov_ref):
    n_chunks = xk_ref.shape[1] // CMP_STRIDE
    half = CMP_BLOCK // 2
    half_cols = half * HEAD_DIM
    kinds = ((xk_ref, kpe_ref, kw1_ref, kw2_ref), (xv_ref, vpe_ref, vw1_ref, vw2_ref))
    top, bot = [], []
    for x_ref, pe_ref, w1_ref, _ in kinds:
        tokens = [x_ref[0, pl.ds(l, n_chunks, stride=CMP_STRIDE), :] for l in range(half)]
        for gi in range(N_KV_GROUPS):
            chunk = jnp.concatenate([t[:, gi * HEAD_DIM:(gi + 1) * HEAD_DIM] for t in tokens],
                                    axis=1)
            top.append(jnp.dot((chunk + pe_ref[0:1, :]).astype(BF16), w1_ref[:half_cols, :],
                               preferred_element_type=F32))
            bot.append(jnp.dot((chunk + pe_ref[1:2, :]).astype(BF16), w1_ref[half_cols:, :],
                               preferred_element_type=F32))
    row = lax.broadcasted_iota(jnp.int32, (n_chunks, HEAD_DIM), 0)
    cmp_end = lax.broadcasted_iota(jnp.int32, (n_chunks, 1), 0) * CMP_STRIDE + (CMP_BLOCK - 1)
    key_tail = _key_tail(cmp_end, n_chunks)
    for s in range(4):
        w2_ref = kinds[s // N_KV_GROUPS][3]
        hidden = top[s] + pltpu.roll(bot[s], n_chunks - 1, axis=0)
        out = jnp.dot(jax.nn.gelu(hidden).astype(BF16), w2_ref[...], preferred_element_type=F32)
        out = jnp.where(row < n_chunks - 1, out, 0.0)
        if s // N_KV_GROUPS == 0:
            ok_ref[0, s % N_KV_GROUPS] = jnp.concatenate([out, key_tail], axis=1).astype(BF16)
        else:
            ov_ref[0, s % N_KV_GROUPS] = _wide_value(out)


def _compress(cmp_in, k_pe, k_w1, k_w2, v_pe, v_w1, v_w2):
    b, t, _ = cmp_in.shape
    n_chunks = t // CMP_STRIDE
    kv_cols = N_KV_GROUPS * HEAD_DIM
    return pl.pallas_call(
        _compress_kernel,
        grid=(b,),
        in_specs=[pl.BlockSpec((1, t, kv_cols), lambda bi: (bi, 0, 0)),
                  pl.BlockSpec((1, t, kv_cols), lambda bi: (bi, 0, 1)),
                  _const_spec(k_pe.shape), _const_spec(k_w1.shape), _const_spec(k_w2.shape),
                  _const_spec(v_pe.shape), _const_spec(v_w1.shape), _const_spec(v_w2.shape)],
        out_specs=[pl.BlockSpec((1, N_KV_GROUPS, n_chunks, AUG_DIM), lambda bi: (bi, 0, 0, 0)),
                   pl.BlockSpec((1, N_KV_GROUPS, n_chunks, WIDE_DIM), lambda bi: (bi, 0, 0, 0))],
        out_shape=[jax.ShapeDtypeStruct((b, N_KV_GROUPS, n_chunks, AUG_DIM), BF16),
                   jax.ShapeDtypeStruct((b, N_KV_GROUPS, n_chunks, WIDE_DIM), BF16)],
        compiler_params=_params("parallel"),
        name="compress",
    )(cmp_in, cmp_in, k_pe, k_w1, k_w2, v_pe, v_w1, v_w2)


_NT = (((1,), (1,)), ((), ()))


def _softmax_numerators(s, bias):
    n_slabs = s.shape[1] // LANES
    probs = []
    for r in range(s.shape[0] // Q_BLOCK):
        sb = s[r * Q_BLOCK:(r + 1) * Q_BLOCK] + bias
        slabs = [sb[:, j * LANES:(j + 1) * LANES] for j in range(n_slabs)]
        lane_max = functools.reduce(jnp.maximum, slabs)
        m = jnp.broadcast_to(jnp.max(lane_max, axis=-1, keepdims=True), (Q_BLOCK, LANES))
        probs.append(jnp.concatenate([jnp.exp2(x - m) for x in slabs], axis=1).astype(BF16))
    return jnp.concatenate(probs, axis=0)


def _pair_scores(q4, k):
    half = q4.shape[0] // 2
    return [lax.dot_general(q4[i * half:(i + 1) * half], k, _NT, preferred_element_type=F32)
            for i in range(2)]


def _pair_attend(scores, v, bias):
    probs = [_softmax_numerators(s, bias) for s in scores]
    outs = [jnp.dot(p, v, preferred_element_type=F32) for p in probs]
    return jnp.concatenate(probs, axis=0), jnp.concatenate(outs, axis=0)


def _for_each_group(n, body):
    def four(i, carry):
        body(4 * i, 2)
        body(4 * i + 2, 2)
        return carry

    lax.fori_loop(0, n // 4, four, 0)
    rest = (n // 4) * 4

    @pl.when(n - rest >= 2)
    def _():
        body(rest, 2)

    @pl.when((n - rest) % 2 == 1)
    def _():
        body(n - 1, 1)


def _block_ranks(score):
    n_blocks, width = score.shape
    n_slabs = n_blocks // SUBLANES
    slabs = [score[s * SUBLANES:(s + 1) * SUBLANES, :] for s in range(n_slabs)]
    ranks = [jnp.zeros((SUBLANES, width), F32) for _ in range(n_slabs)]
    sub = lax.broadcasted_iota(jnp.int32, (SUBLANES, width), 0)
    for j in range(n_blocks):
        other = jnp.broadcast_to(score[j:j + 1, :], (SUBLANES, width))
        for s in range(n_slabs):
            if s * SUBLANES > j:
                ahead = jnp.where(other >= slabs[s], 1.0, 0.0)
            elif (s + 1) * SUBLANES - 1 <= j:
                ahead = jnp.where(other > slabs[s], 1.0, 0.0)
            else:
                ahead = jnp.where(sub > j - s * SUBLANES, jnp.where(other >= slabs[s], 1.0, 0.0),
                                  jnp.where(other > slabs[s], 1.0, 0.0))
            ranks[s] = ranks[s] + ahead
    return jnp.concatenate(ranks, axis=0)


def _attn_kernel(q_ref, slope_ref, kc_ref, vc_ref, ks_ref, vs_ref, kw_ref, vw_ref, gate_ref,
                 mcs_t_ref, expand_ref, chunk_of_ref, win_bias_ref, o_ref,
                 q4_ref, chosen_ref, s_ref, m_ref, acc_ref, gate_b_ref, slot_ref):
    qb = pl.program_id(2)
    q0 = qb * Q_BLOCK
    rows = HEADS_PER_GROUP * Q_BLOCK
    n_cmp = kc_ref.shape[2]
    n_sel = expand_ref.shape[1]

    qblk = q_ref[0]
    slope_cols = slope_ref[0]
    q4 = jnp.concatenate(
        [jnp.concatenate([qblk[:, r * HEAD_DIM:(r + 1) * HEAD_DIM],
                          jnp.broadcast_to(slope_cols[r:r + 1, :], (Q_BLOCK, HEAD_DIM))], axis=1)
         for r in range(HEADS_PER_GROUP)], axis=0)
    t_row = q0 + (lax.broadcasted_iota(jnp.int32, (rows, LANES), 0) & (Q_BLOCK - 1))
    tq = q0 + lax.broadcasted_iota(jnp.int32, (Q_BLOCK, 1), 0)

    cmp_end = lax.broadcasted_iota(jnp.int32, (1, n_cmp), 1) * CMP_STRIDE + (CMP_BLOCK - 1)
    pc4, ov = _pair_attend(_pair_scores(q4, kc_ref[0, 0]), vc_ref[0, 0],
                           jnp.where(tq >= cmp_end, 0.0, NEG_BIG))
    o_cmp = jnp.where(t_row >= CMP_BLOCK - 1, ov[:, :LANES] / ov[:, LANES:], 0.0)

    w0 = pl.multiple_of(jnp.maximum(q0 - WINDOW, 0), Q_BLOCK)
    win_scores = _pair_scores(q4, kw_ref[0, 0, pl.ds(w0, WIN_KEYS), :])

    imp_l = lax.dot_general(mcs_t_ref[...], pc4, _NT, preferred_element_type=F32)
    imp = None
    for r in range(HEADS_PER_GROUP):
        cs = slice(r * Q_BLOCK, (r + 1) * Q_BLOCK)
        part = imp_l[:n_sel, cs] / imp_l[n_sel:n_sel + 1, cs]
        imp = part if imp is None else imp + part
    blk = lax.broadcasted_iota(jnp.int32, (n_sel, Q_BLOCK), 0)
    tq_l = q0 + lax.broadcasted_iota(jnp.int32, (n_sel, Q_BLOCK), 1)
    cur = tq_l >> 6
    forced = (blk == 0) | (blk == cur) | (blk == cur - 1)
    valid = blk * SEL_BLOCK <= tq_l
    score = jnp.where(forced, FORCED_SCORE, jnp.where(valid, imp, -1.0))
    rank = _block_ranks(score)
    chosen_t = jnp.where(valid, jnp.where(rank < float(min(SEL_TOPN, n_sel)), 1.0, 0.0), 0.0)
    chosen_ref[...] = chosen_t.T.astype(BF16)
    q4_ref[...] = q4

    n_chunks = expand_ref.shape[0]
    per_chunk = jnp.dot(chunk_of_ref[...], chosen_t.astype(BF16), preferred_element_type=F32)
    live = jnp.max(per_chunk, axis=1, keepdims=True) > 0.0
    weight = (1 << lax.broadcasted_iota(jnp.int32, (n_chunks, 1), 0)).astype(F32)
    live_bits = jnp.sum(jnp.where(live, weight, 0.0)).astype(jnp.int32)

    n_live = 0
    for c in range(n_chunks):
        slot_ref[n_live] = c
        n_live = n_live + ((live_bits >> c) & 1)

    vwin = vw_ref[0, 0, pl.ds(w0, WIN_KEYS), :]
    bias_w = win_bias_ref[jnp.minimum(qb, WINDOW // Q_BLOCK)]
    _, wv = _pair_attend(win_scores, vwin, bias_w)
    o_win = wv[:, :LANES] / wv[:, LANES:]

    for r in range(HEADS_PER_GROUP):
        for j in range(3):
            gate_b_ref[3 * r + j] = jnp.broadcast_to(gate_ref[0, :, 3 * r + j:3 * r + j + 1],
                                                     (Q_BLOCK, LANES))

    m_ref[...] = jnp.full(m_ref.shape, NEG_BIG, F32)

    def score_body(slot0, count):
        lane_max = [None] * HEADS_PER_GROUP
        starts, biases = [], []
        for u in range(count):
            c = slot_ref[slot0 + u]
            k0 = pl.multiple_of(c * SEL_CHUNK, SEL_CHUNK)
            hit = jnp.dot(chosen_ref[...], expand_ref[c], preferred_element_type=F32)
            pos = k0 + lax.broadcasted_iota(jnp.int32, (1, SEL_CHUNK), 1)
            biases.append(jnp.where(pos <= tq, (hit - 1.0) * (-NEG_BIG), NEG_BIG))
            starts.append(k0)
        for u in range(count):
            k = ks_ref[0, 0, pl.ds(starts[u], SEL_CHUNK), :]
            if count == 1:
                parts = _pair_scores(q4_ref[...], k)
            else:
                parts = [lax.dot_general(q4_ref[...], k, _NT, preferred_element_type=F32)]
            heads_per_part = HEADS_PER_GROUP // len(parts)
            for r in range(HEADS_PER_GROUP):
                lo = (r % heads_per_part) * Q_BLOCK
                s = parts[r // heads_per_part][lo:lo + Q_BLOCK] + biases[u]
                s_ref[slot0 + u, r * Q_BLOCK:(r + 1) * Q_BLOCK, :] = s
                mx = jnp.maximum(s[:, :LANES], s[:, LANES:])
                lane_max[r] = mx if lane_max[r] is None else jnp.maximum(lane_max[r], mx)
        for r in range(HEADS_PER_GROUP):
            rs = slice(r * Q_BLOCK, (r + 1) * Q_BLOCK)
            m_ref[rs, :] = jnp.maximum(m_ref[rs, :], lane_max[r])

    _for_each_group(n_live, score_body)
    m_ref[...] = jnp.broadcast_to(jnp.max(m_ref[...], axis=-1, keepdims=True), m_ref.shape)

    acc_ref[...] = jnp.zeros(acc_ref.shape, F32)

    def value_body(slot0, count):
        pv = None
        for u in range(count):
            slot = slot0 + u
            v0 = pl.multiple_of(slot_ref[slot] * SEL_CHUNK, SEL_CHUNK)
            v = vs_ref[0, 0, pl.ds(v0, SEL_CHUNK), :]
            p = jnp.concatenate([jnp.exp2(s_ref[slot, :, :LANES] - m_ref[...]),
                                 jnp.exp2(s_ref[slot, :, LANES:] - m_ref[...])],
                                axis=1).astype(BF16)
            if count == 1:
                half = rows // 2
                d = jnp.concatenate([jnp.dot(p[:half], v, preferred_element_type=F32),
                                     jnp.dot(p[half:], v, preferred_element_type=F32)], axis=0)
            else:
                d = jnp.dot(p, v, preferred_element_type=F32)
            pv = d if pv is None else pv + d
        acc_ref[...] += pv

    _for_each_group(n_live, value_body)
    o_sel = acc_ref[:, :LANES] / acc_ref[:, LANES:]

    lane = lax.broadcasted_iota(jnp.int32, (Q_BLOCK, LANES), 1)
    gated = []
    for r in range(HEADS_PER_GROUP):
        rs = slice(r * Q_BLOCK, (r + 1) * Q_BLOCK)
        gated.append(gate_b_ref[3 * r] * o_cmp[rs] + gate_b_ref[3 * r + 1] * o_sel[rs]
                     + gate_b_ref[3 * r + 2] * o_win[rs])
    o_ref[0] = jnp.concatenate(
        [jnp.where(lane < HEAD_DIM, gated[2 * i], gated[2 * i + 1])
         for i in range(HEADS_PER_GROUP // 2)], axis=-1)


def _attention(q, cmp_k, cmp_v, keys, values, gates):
    b, t, _ = q.shape
    n_cmp = cmp_k.shape[2]
    n_sel = t // SEL_BLOCK
    n_real_cmp = (t - CMP_BLOCK) // CMP_STRIDE + 1

    cs = np.arange(n_cmp)[None, :] * CMP_STRIDE
    js = np.arange(n_sel)[:, None] * SEL_BLOCK
    ov = np.minimum(cs + CMP_BLOCK, js + SEL_BLOCK) - np.maximum(cs, js)
    mcs_t = np.zeros((n_sel + SUBLANES, n_cmp), np.float32)
    mcs_t[:n_sel] = np.maximum(ov, 0).astype(np.float32) / CMP_BLOCK
    mcs_t[:n_sel, n_real_cmp:] = 0.0
    mcs_t[n_sel] = 1.0
    key_blk = (np.arange(t) // SEL_BLOCK).reshape(t // SEL_CHUNK, 1, SEL_CHUNK)
    expand = (key_blk == np.arange(n_sel)[None, :, None]).astype(np.float32)
    n_chunks = t // SEL_CHUNK
    chunk_of = (np.arange(n_sel)[None, :] // (SEL_CHUNK // SEL_BLOCK)
                == np.arange(n_chunks)[:, None]).astype(np.float32)
    rows = HEADS_PER_GROUP * Q_BLOCK
    dist = (np.arange(WINDOW // Q_BLOCK + 1)[:, None, None] * Q_BLOCK
            + np.arange(Q_BLOCK)[None, :, None] - np.arange(WIN_KEYS)[None, None, :])
    win_bias = np.where((dist >= 0) & (dist < WINDOW), 0.0, NEG_BIG).astype(np.float32)

    slope_l2 = LOG2E * np.power(2.0, -8.0 * np.arange(1, N_Q_HEADS + 1) / N_Q_HEADS)
    hi = slope_l2.astype(BF16).astype(np.float64)
    lo = (slope_l2 - hi).astype(BF16).astype(np.float64)
    cols = np.zeros((N_KV_GROUPS, SUBLANES, HEAD_DIM), np.float32)
    heads = cols[:, :HEADS_PER_GROUP].reshape(N_Q_HEADS, HEAD_DIM)
    heads[:, 0], heads[:, 1], heads[:, 2], heads[:, 3] = SEL_BLOCK * hi, SEL_BLOCK * lo, hi, lo
    cols[:, :HEADS_PER_GROUP] = heads.reshape(N_KV_GROUPS, HEADS_PER_GROUP, HEAD_DIM)

    def kv_spec(branch, rows, width):
        return pl.BlockSpec((1, 1, rows, width),
                            lambda bi, gi, qi, branch=branch: (bi, branch * N_KV_GROUPS + gi, 0, 0))

    return pl.pallas_call(
        _attn_kernel,
        grid=(b, N_KV_GROUPS, t // Q_BLOCK),
        in_specs=[
            pl.BlockSpec((1, Q_BLOCK, GROUP_WIDTH), lambda bi, gi, qi: (bi, qi, gi)),
            pl.BlockSpec((1, SUBLANES, HEAD_DIM), lambda bi, gi, qi: (gi, 0, 0)),
            kv_spec(0, n_cmp, AUG_DIM), kv_spec(0, n_cmp, WIDE_DIM),
            kv_spec(0, t, AUG_DIM), kv_spec(0, t, WIDE_DIM),
            kv_spec(1, t, AUG_DIM), kv_spec(1, t, WIDE_DIM),
            pl.BlockSpec((1, Q_BLOCK, LANES), lambda bi, gi, qi: (bi, qi, gi)),
            _const_spec(mcs_t.shape), _const_spec(expand.shape), _const_spec(chunk_of.shape),
            _const_spec(win_bias.shape),
        ],
        out_specs=pl.BlockSpec((1, Q_BLOCK, GROUP_WIDTH), lambda bi, gi, qi: (bi, qi, gi)),
        out_shape=jax.ShapeDtypeStruct((b, t, ATTN_WIDTH), F32),
        scratch_shapes=[
            pltpu.VMEM((rows, AUG_DIM), BF16),
            pltpu.VMEM((Q_BLOCK, n_sel), BF16),
            pltpu.VMEM((n_chunks, rows, SEL_CHUNK), F32),
            pltpu.VMEM((rows, LANES), F32),
            pltpu.VMEM((rows, WIDE_DIM), F32),
            pltpu.VMEM((3 * HEADS_PER_GROUP, Q_BLOCK, LANES), F32),
            pltpu.SMEM((n_chunks + 1,), jnp.int32),
        ],
        compiler_params=_params("parallel", "parallel", "arbitrary"),
        name="nsa_attn",
    )(q, jnp.asarray(cols, BF16), cmp_k, cmp_v, keys, values, keys, values, gates,
      jnp.asarray(mcs_t, BF16), jnp.asarray(expand, BF16), jnp.asarray(chunk_of, BF16),
      jnp.asarray(win_bias))


def _lru_kernel(xr_ref, xg_ref, cw_ref, cb_ref, wa_ref, ba_ref, wx_ref, bx_ref, lam_ref,
                g_ref, o_ref, xs_ref, a_ref, b_ref, h_ref):
    ti = pl.program_id(1)
    tt = xr_ref.shape[1]

    @pl.when(ti == 0)
    def _():
        xs_ref[...] = jnp.zeros_like(xs_ref)
        h_ref[...] = jnp.zeros_like(h_ref)

    n_groups = tt // SUBLANES
    as_groups = lambda v: v.reshape(n_groups, SUBLANES, LRU_WIDTH)
    sub = lax.broadcasted_iota(jnp.int32, (n_groups, SUBLANES, LRU_WIDTH), 1)

    x = as_groups(xr_ref[0])
    last = xs_ref[...]
    xc = cb_ref[...] + cw_ref[CONV_WIDTH - 1:CONV_WIDTH, :] * x
    for j in range(CONV_WIDTH - 1):
        lag = CONV_WIDTH - 1 - j
        rot = pltpu.roll(x, lag, axis=1)
        rot_before = jnp.concatenate([pltpu.roll(last, lag, axis=0)[None], rot[:-1]], axis=0)
        xc = xc + cw_ref[j:j + 1, :] * jnp.where(sub < lag, rot_before, rot)
    xs_ref[...] = x[n_groups - 1]
    xc = xc.reshape(tt, LRU_WIDTH)

    xb = xc.astype(BF16)
    r = jax.nn.sigmoid(jnp.dot(xb, wa_ref[...], preferred_element_type=F32) + ba_ref[...])
    i = jax.nn.sigmoid(jnp.dot(xb, wx_ref[...], preferred_element_type=F32) + bx_ref[...])
    neg_lam = -lam_ref[...]
    softplus = jnp.maximum(neg_lam, 0.0) + jnp.log1p(jnp.exp(-jnp.abs(neg_lam)))
    log_a = -LRU_C * r * softplus
    a = jnp.exp(log_a)
    th = jnp.tanh(log_a)
    mult = jnp.sqrt(jnp.maximum(-2.0 * th / (1.0 - th), 0.0))
    b = mult * i * xc

    a, b = as_groups(a), as_groups(b)
    for s in (1, 2, 4):
        ok = sub >= s
        a_prev = jnp.where(ok, pltpu.roll(a, s, axis=1), 1.0)
        b_prev = jnp.where(ok, pltpu.roll(b, s, axis=1), 0.0)
        b = a * b_prev + b
        a = a * a_prev
    a_ref[...] = a.reshape(tt, LRU_WIDTH)
    b_ref[...] = b.reshape(tt, LRU_WIDTH)

    def group_body(k, h):
        r0 = pl.multiple_of(k * SUBLANES, SUBLANES)
        h8 = a_ref[pl.ds(r0, SUBLANES), :] * h + b_ref[pl.ds(r0, SUBLANES), :]
        b_ref[pl.ds(r0, SUBLANES), :] = h8
        return jnp.broadcast_to(h8[SUBLANES - 1:SUBLANES, :], (SUBLANES, LRU_WIDTH))

    h_ref[...] = lax.fori_loop(0, tt // SUBLANES, group_body, h_ref[...])
    out = b_ref[...] * jax.nn.gelu(xg_ref[0])
    o_ref[0] = _rms(out, g_ref[...]).astype(BF16)


def _lru(xr, xg, conv_w, conv_b, wa, ba, wx, bx, lam, g):
    b, t, c = xr.shape
    tt = LRU_TOKENS
    tok = pl.BlockSpec((1, tt, c), lambda bi, ti: (bi, ti, 0))
    vec = _const_spec((1, c))
    return pl.pallas_call(
        _lru_kernel,
        grid=(b, t // tt),
        in_specs=[tok, tok, _const_spec((CONV_WIDTH, c)), vec, _const_spec((c, c)), vec,
                  _const_spec((c, c)), vec, vec, vec],
        out_specs=tok,
        out_shape=jax.ShapeDtypeStruct((b, t, c), BF16),
        scratch_shapes=[pltpu.VMEM((SUBLANES, c), F32), pltpu.VMEM((tt, c), F32),
                        pltpu.VMEM((tt, c), F32), pltpu.VMEM((SUBLANES, c), F32)],
        compiler_params=_params("parallel", "arbitrary"),
        name="rg_lru",
    )(xr, xg, conv_w, conv_b, wa, ba, wx, bx, lam, g)


def _mix_ffn_kernel(h_ref, attn_ref, lru_ref, ga_ref, mix_g_ref, wo_a_ref, wo_l_ref,
                    pre_g_ref, post_g_ref, wg_ref, wu_ref, wd_ref, o_ref, act_ref):
    ya = _rms(attn_ref[...], ga_ref[...]).astype(BF16)
    m = (jnp.dot(ya, wo_a_ref[...], preferred_element_type=F32)
         + jnp.dot(lru_ref[...], wo_l_ref[...], preferred_element_type=F32))
    h = h_ref[...] + _rms(m, mix_g_ref[...])
    o_ref[...] = _ffn_half_step(h, pre_g_ref, post_g_ref, wg_ref, wu_ref, wd_ref, act_ref)


def _mix_ffn(h, attn, lru, attn_g, mix_g, wo_a, wo_l, ffn_weights):
    n = h.shape[0]
    tm = FFN_TOKENS

    def tok(width):
        return pl.BlockSpec((tm, width), lambda i: (i, 0))

    return pl.pallas_call(
        _mix_ffn_kernel,
        grid=(n // tm,),
        in_specs=[tok(D_MODEL), tok(ATTN_WIDTH), tok(LRU_WIDTH), _const_spec((1, ATTN_WIDTH)),
                  _const_spec((1, D_MODEL)), _const_spec((ATTN_WIDTH, D_MODEL)),
                  _const_spec((LRU_WIDTH, D_MODEL))] + _ffn_weight_specs(),
        out_specs=tok(D_MODEL),
        out_shape=jax.ShapeDtypeStruct((n, D_MODEL), F32),
        scratch_shapes=[pltpu.VMEM((tm, D_FF), BF16)],
        compiler_params=_params("parallel"),
        name="mix_ffn",
    )(h, attn, lru, attn_g, mix_g, wo_a, wo_l, *ffn_weights)


def _pack_w_in(w_in):
    kv_cols = N_KV_GROUPS * HEAD_DIM
    gate_lo = ATTN_WIDTH + 6 * kv_cols
    gate_hi = gate_lo + 3 * N_Q_HEADS
    per_group = 3 * HEADS_PER_GROUP
    w_in = w_in.astype(BF16)
    pad = jnp.zeros((w_in.shape[0], LANES - per_group), w_in.dtype)
    gate_slabs = []
    for gi in range(N_KV_GROUPS):
        gate_slabs += [w_in[:, gate_lo + gi * per_group:gate_lo + (gi + 1) * per_group], pad]
    return jnp.concatenate([w_in[:, :gate_lo]] + gate_slabs + [w_in[:, gate_hi:]], axis=1)


def _block_diag(w):
    nb, d, e = w.shape
    eye = jnp.eye(nb, dtype=w.dtype)
    return jnp.einsum("nde,nm->ndme", w, eye).reshape(nb * d, nb * e).astype(BF16)


def _layer(h, p):
    b, t, d = h.shape
    n = b * t
    row = lambda v: v.reshape(1, -1)

    def ffn_weights(i):
        return (row(p[f"ffn{i}_pre_g"]), row(p[f"ffn{i}_post_g"]), p[f"ffn{i}_w_gate"].astype(BF16),
                p[f"ffn{i}_w_up"].astype(BF16), p[f"ffn{i}_w_down"].astype(BF16))

    h1, q, cmp_in, keys, values, gates, xr, xg = _ffn_proj(
        h, ffn_weights(1), row(p["mix_pre_g"]), _pack_w_in(p["w_in"]))
    half_block = lambda pe: pe.reshape(2, (CMP_BLOCK // 2) * HEAD_DIM)
    cmp_k, cmp_v = _compress(cmp_in, half_block(p["cmp_k_pe"]), p["cmp_k_w1"].astype(BF16),
                             p["cmp_k_w2"].astype(BF16), half_block(p["cmp_v_pe"]),
                             p["cmp_v_w1"].astype(BF16), p["cmp_v_w2"].astype(BF16))
    attn = _attention(q, cmp_k, cmp_v, keys, values, gates)
    lru = _lru(xr, xg, p["conv_w"], row(p["conv_b"]), _block_diag(p["lru_w_a"]),
               row(p["lru_b_a"]), _block_diag(p["lru_w_x"]), row(p["lru_b_x"]),
               row(p["lru_lambda"]), row(p["lru_out_g"]))
    w_out = p["w_out"].astype(BF16)
    h3 = _mix_ffn(h1.reshape(n, d), attn.reshape(n, ATTN_WIDTH), lru.reshape(n, LRU_WIDTH),
                  row(p["attn_out_g"]), row(p["mix_post_g"]),
                  w_out[:ATTN_WIDTH], w_out[ATTN_WIDTH:], ffn_weights(2))
    return h3.reshape(b, t, d)


_PARAM_NAMES = (
    "ffn1_pre_g", "ffn1_post_g", "ffn1_w_gate", "ffn1_w_up", "ffn1_w_down",
    "mix_pre_g", "mix_post_g", "w_in", "cmp_k_pe", "cmp_k_w1", "cmp_k_w2",
    "cmp_v_pe", "cmp_v_w1", "cmp_v_w2", "conv_w", "conv_b", "lru_w_a", "lru_b_a",
    "lru_w_x", "lru_b_x", "lru_lambda", "attn_out_g", "lru_out_g", "w_out",
    "ffn2_pre_g", "ffn2_post_g", "ffn2_w_gate", "ffn2_w_up", "ffn2_w_down",
)


def kernel(x, ffn1_pre_g, ffn1_post_g, ffn1_w_gate, ffn1_w_up, ffn1_w_down, mix_pre_g, mix_post_g, w_in, cmp_k_pe, cmp_k_w1, cmp_k_w2, cmp_v_pe, cmp_v_w1, cmp_v_w2, conv_w, conv_b, lru_w_a, lru_b_a, lru_w_x, lru_b_x, lru_lambda, attn_out_g, lru_out_g, w_out, ffn2_pre_g, ffn2_post_g, ffn2_w_gate, ffn2_w_up, ffn2_w_down):
    stacked = dict(zip(_PARAM_NAMES, (
        ffn1_pre_g, ffn1_post_g, ffn1_w_gate, ffn1_w_up, ffn1_w_down, mix_pre_g, mix_post_g,
        w_in, cmp_k_pe, cmp_k_w1, cmp_k_w2, cmp_v_pe, cmp_v_w1, cmp_v_w2, conv_w, conv_b,
        lru_w_a, lru_b_a, lru_w_x, lru_b_x, lru_lambda, attn_out_g, lru_out_g, w_out,
        ffn2_pre_g, ffn2_post_g, ffn2_w_gate, ffn2_w_up, ffn2_w_down)))
    h = x
    for layer in range(ffn1_pre_g.shape[0]):
        h = _layer(h, {k: v[layer] for k, v in stacked.items()})
    return h
```

```python
import functools

import numpy as np
import jax
import jax.numpy as jnp
from jax import lax
from jax.experimental import pallas as pl
from jax.experimental.pallas import tpu as pltpu

F32 = jnp.float32
BF16 = jnp.bfloat16

D_MODEL = 1024
N_Q_HEADS = 8
HEAD_DIM = 64
N_KV_GROUPS = 2
HEADS_PER_GROUP = N_Q_HEADS // N_KV_GROUPS
ATTN_WIDTH = N_Q_HEADS * HEAD_DIM
GROUP_WIDTH = HEADS_PER_GROUP * HEAD_DIM
CMP_BLOCK = 32
CMP_STRIDE = 16
CMP_HIDDEN = 256
SEL_BLOCK = 64
SEL_TOPN = 16
WINDOW = 512
Q_BLOCK = 256
LRU_WIDTH = 512
LRU_BLOCKS = 8
CONV_WIDTH = 4
LRU_C = 8.0
D_FF = 2816
NORM_EPS = 1e-6

LANES = 128
SUBLANES = 8
VMEM_LIMIT_BYTES = 56 * 1024 * 1024

NEG_BIG = -1e30
FORCED_SCORE = 3e38
LOG2E = 1.4426950408889634
AUG_DIM = 2 * HEAD_DIM
WIDE_DIM = 4 * HEAD_DIM

FFN_TOKENS = 512
FFN_CHUNK = 256
LRU_TOKENS = 1024
SEL_CHUNK = 256
WIN_KEYS = WINDOW + Q_BLOCK

COL_Q = 0
COL_CMP = COL_Q + ATTN_WIDTH
COL_KV = COL_CMP + 2 * N_KV_GROUPS * HEAD_DIM
COL_GATE = COL_KV + 4 * N_KV_GROUPS * HEAD_DIM
COL_XR = COL_GATE + N_KV_GROUPS * LANES
COL_XG = COL_XR + LRU_WIDTH
PROJ_WIDTH = COL_XG + LRU_WIDTH


def _rms(x, g):
    ms = jnp.mean(x * x, axis=-1, keepdims=True)
    return x * lax.rsqrt(ms + NORM_EPS) * g


def _const_spec(shape):
    nd = len(shape)
    return pl.BlockSpec(shape, lambda *_: (0,) * nd, pipeline_mode=pl.Buffered(1))


def _params(*sem):
    return pltpu.CompilerParams(dimension_semantics=sem, vmem_limit_bytes=VMEM_LIMIT_BYTES)


def _ffn_half_step(x, pre_g_ref, post_g_ref, wg_ref, wu_ref, wd_ref, act_ref):
    xb = _rms(x, pre_g_ref[...]).astype(BF16)
    for c in range(D_FF // FFN_CHUNK):
        sl = slice(c * FFN_CHUNK, (c + 1) * FFN_CHUNK)
        gate = jnp.dot(xb, wg_ref[:, sl], preferred_element_type=F32)
        up = jnp.dot(xb, wu_ref[:, sl], preferred_element_type=F32)
        act_ref[:, sl] = (jax.nn.silu(gate) * up).astype(BF16)
    f = jnp.dot(act_ref[...], wd_ref[...], preferred_element_type=F32)
    return x + 0.5 * _rms(f, post_g_ref[...])


def _ffn_weight_specs():
    return [_const_spec((1, D_MODEL)), _const_spec((1, D_MODEL)), _const_spec((D_MODEL, D_FF)),
            _const_spec((D_MODEL, D_FF)), _const_spec((D_FF, D_MODEL))]


def _key_tail(pos, rows):
    lane = lax.broadcasted_iota(jnp.int32, (rows, HEAD_DIM), 1)
    hi = (pos >> 6).astype(F32)
    lo = (pos & (SEL_BLOCK - 1)).astype(F32)
    return jnp.where(lane < 2, hi, jnp.where(lane < 4, lo, 0.0))


def _wide_value(v):
    return jnp.concatenate([v, v, jnp.ones((v.shape[0], LANES), F32)], axis=1).astype(BF16)


def _ffn_proj_kernel(x_ref, pre_g_ref, post_g_ref, wg_ref, wu_ref, wd_ref, g_ref, w_ref,
                     h_ref, q_ref, cmp_ref, k_ref, v_ref, gate_ref, xr_ref, xg_ref, act_ref):
    tm = x_ref.shape[1]
    h = _ffn_half_step(x_ref[0], pre_g_ref, post_g_ref, wg_ref, wu_ref, wd_ref, act_ref)
    h_ref[0] = h
    hb = _rms(h, g_ref[...]).astype(BF16)
    p = jnp.dot(hb, w_ref[...], preferred_element_type=F32)
    q_ref[0] = (p[:, COL_Q:COL_CMP] * (HEAD_DIM ** -0.5 * LOG2E)).astype(BF16)
    cmp_ref[0] = p[:, COL_CMP:COL_KV]
    pos = pl.program_id(1) * tm + lax.broadcasted_iota(jnp.int32, (tm, 1), 0)
    key_tail = _key_tail(pos, tm)
    for i in range(4 * N_KV_GROUPS):
        lo = COL_KV + i * HEAD_DIM
        x = p[:, lo:lo + HEAD_DIM]
        branch, is_value, gi = i // (2 * N_KV_GROUPS), (i // N_KV_GROUPS) % 2, i % N_KV_GROUPS
        if is_value:
            v_ref[0, branch * N_KV_GROUPS + gi] = _wide_value(x)
        else:
            k_ref[0, branch * N_KV_GROUPS + gi] = jnp.concatenate([x, key_tail], axis=1).astype(BF16)
    gate_ref[0] = jax.nn.sigmoid(p[:, COL_GATE:COL_XR])
    xr_ref[0] = p[:, COL_XR:COL_XG]
    xg_ref[0] = p[:, COL_XG:PROJ_WIDTH]


def _ffn_proj(x, ffn_weights, g, w_packed):
    b, t, _ = x.shape
    tm = FFN_TOKENS

    def tok(width):
        return pl.BlockSpec((1, tm, width), lambda bi, ti: (bi, ti, 0))

    return pl.pallas_call(
        _ffn_proj_kernel,
        grid=(b, t // tm),
        in_specs=[tok(D_MODEL)] + _ffn_weight_specs()
        + [_const_spec((1, D_MODEL)), _const_spec((D_MODEL, PROJ_WIDTH))],
        out_specs=[
            tok(D_MODEL),
            tok(ATTN_WIDTH),
            tok(2 * N_KV_GROUPS * HEAD_DIM),
            pl.BlockSpec((1, 2 * N_KV_GROUPS, tm, AUG_DIM), lambda bi, ti: (bi, 0, ti, 0)),
            pl.BlockSpec((1, 2 * N_KV_GROUPS, tm, WIDE_DIM), lambda bi, ti: (bi, 0, ti, 0)),
            tok(N_KV_GROUPS * LANES),
            tok(LRU_WIDTH),
            tok(LRU_WIDTH),
        ],
        out_shape=[
            jax.ShapeDtypeStruct((b, t, D_MODEL), F32),
            jax.ShapeDtypeStruct((b, t, ATTN_WIDTH), BF16),
            jax.ShapeDtypeStruct((b, t, 2 * N_KV_GROUPS * HEAD_DIM), F32),
            jax.ShapeDtypeStruct((b, 2 * N_KV_GROUPS, t, AUG_DIM), BF16),
            jax.ShapeDtypeStruct((b, 2 * N_KV_GROUPS, t, WIDE_DIM), BF16),
            jax.ShapeDtypeStruct((b, t, N_KV_GROUPS * LANES), F32),
            jax.ShapeDtypeStruct((b, t, LRU_WIDTH), F32),
            jax.ShapeDtypeStruct((b, t, LRU_WIDTH), F32),
        ],
        scratch_shapes=[pltpu.VMEM((tm, D_FF), BF16)],
        compiler_params=_params("parallel", "parallel"),
        name="ffn_proj",
    )(x, *ffn_weights, g, w_packed)


def _compress_kernel(xk_ref, xv_ref, kpe_ref, kw1_ref, kw2_ref, vpe_ref, vw1_ref, vw2_ref,
                     ok_ref, ov_ref):
    n_chunks = xk_ref.shape[1] // CMP_STRIDE
    half = CMP_BLOCK // 2
    half_cols = half * HEAD_DIM
    kinds = ((xk_ref, kpe_ref, kw1_ref, kw2_ref), (xv_ref, vpe_ref, vw1_ref, vw2_ref))
    top, bot = [], []
    for x_ref, pe_ref, w1_ref, _ in kinds:
        tokens = [x_ref[0, pl.ds(l, n_chunks, stride=CMP_STRIDE), :] for l in range(half)]
        for gi in range(N_KV_GROUPS):
            chunk = jnp.concatenate([t[:, gi * HEAD_DIM:(gi + 1) * HEAD_DIM] for t in tokens],
                                    axis=1)
            top.append(jnp.dot((chunk + pe_ref[0:1, :]).astype(BF16), w1_ref[:half_cols, :],
                               preferred_element_type=F32))
            bot.append(jnp.dot((chunk + pe_ref[1:2, :]).astype(BF16), w1_ref[half_cols:, :],
                               preferred_element_type=F32))
    row = lax.broadcasted_iota(jnp.int32, (n_chunks, HEAD_DIM), 0)
    cmp_end = lax.broadcasted_iota(jnp.int32, (n_chunks, 1), 0) * CMP_STRIDE + (CMP_BLOCK - 1)
    key_tail = _key_tail(cmp_end, n_chunks)
    for s in range(4):
        w2_ref = kinds[s // N_KV_GROUPS][3]
        hidden = top[s] + pltpu.roll(bot[s], n_chunks - 1, axis=0)
        out = jnp.dot(jax.nn.gelu(hidden).astype(BF16), w2_ref[...], preferred_element_type=F32)
        out = jnp.where(row < n_chunks - 1, out, 0.0)
        if s // N_KV_GROUPS == 0:
            ok_ref[0, s % N_KV_GROUPS] = jnp.concatenate([out, key_tail], axis=1).astype(BF16)
        else:
            ov_ref[0, s % N_KV_GROUPS] = _wide_value(out)


def _compress(cmp_in, k_pe, k_w1, k_w2, v_pe, v_w1, v_w2):
    b, t, _ = cmp_in.shape
    n_chunks = t // CMP_STRIDE
    kv_cols = N_KV_GROUPS * HEAD_DIM
    return pl.pallas_call(
        _compress_kernel,
        grid=(b,),
        in_specs=[pl.BlockSpec((1, t, kv_cols), lambda bi: (bi, 0, 0)),
                  pl.BlockSpec((1, t, kv_cols), lambda bi: (bi, 0, 1)),
                  _const_spec(k_pe.shape), _const_spec(k_w1.shape), _const_spec(k_w2.shape),
                  _const_spec(v_pe.shape), _const_spec(v_w1.shape), _const_spec(v_w2.shape)],
        out_specs=[pl.BlockSpec((1, N_KV_GROUPS, n_chunks, AUG_DIM), lambda bi: (bi, 0, 0, 0)),
                   pl.BlockSpec((1, N_KV_GROUPS, n_chunks, WIDE_DIM), lambda bi: (bi, 0, 0, 0))],
        out_shape=[jax.ShapeDtypeStruct((b, N_KV_GROUPS, n_chunks, AUG_DIM), BF16),
                   jax.ShapeDtypeStruct((b, N_KV_GROUPS, n_chunks, WIDE_DIM), BF16)],
        compiler_params=_params("parallel"),
        name="compress",
    )(cmp_in, cmp_in, k_pe, k_w1, k_w2, v_pe, v_w1, v_w2)


_NT = (((1,), (1,)), ((), ()))


def _softmax_numerators(s, bias):
    n_slabs = s.shape[1] // LANES
    probs = []
    for r in range(s.shape[0] // Q_BLOCK):
        sb = s[r * Q_BLOCK:(r + 1) * Q_BLOCK] + bias
        slabs = [sb[:, j * LANES:(j + 1) * LANES] for j in range(n_slabs)]
        lane_max = functools.reduce(jnp.maximum, slabs)
        m = jnp.broadcast_to(jnp.max(lane_max, axis=-1, keepdims=True), (Q_BLOCK, LANES))
        probs.append(jnp.concatenate([jnp.exp2(x - m) for x in slabs], axis=1).astype(BF16))
    return jnp.concatenate(probs, axis=0)


def _pair_scores(q4, k):
    half = q4.shape[0] // 2
    return [lax.dot_general(q4[i * half:(i + 1) * half], k, _NT, preferred_element_type=F32)
            for i in range(2)]


def _pair_attend(scores, v, bias):
    probs = [_softmax_numerators(s, bias) for s in scores]
    outs = [jnp.dot(p, v, preferred_element_type=F32) for p in probs]
    return jnp.concatenate(probs, axis=0), jnp.concatenate(outs, axis=0)


def _for_each_group(n, body):
    def four(i, carry):
        body(4 * i, 2)
        body(4 * i + 2, 2)
        return carry

    lax.fori_loop(0, n // 4, four, 0)
    rest = (n // 4) * 4

    @pl.when(n - rest >= 2)
    def _():
        body(rest, 2)

    @pl.when((n - rest) % 2 == 1)
    def _():
        body(n - 1, 1)


def _block_ranks(score, side_work=()):
    n_blocks, width = score.shape
    n_slabs = n_blocks // SUBLANES
    slabs = [score[s * SUBLANES:(s + 1) * SUBLANES, :] for s in range(n_slabs)]
    ranks = [jnp.zeros((SUBLANES, width), F32) for _ in range(n_slabs)]
    sub = lax.broadcasted_iota(jnp.int32, (SUBLANES, width), 0)
    every = n_blocks // max(len(side_work), 1)
    side_results = []
    for j in range(n_blocks):
        if side_work and j % every == 0 and j // every < len(side_work):
            side_results.append(side_work[j // every]())
        other = jnp.broadcast_to(score[j:j + 1, :], (SUBLANES, width))
        for s in range(n_slabs):
            if s * SUBLANES > j:
                ahead = jnp.where(other >= slabs[s], 1.0, 0.0)
            elif (s + 1) * SUBLANES - 1 <= j:
                ahead = jnp.where(other > slabs[s], 1.0, 0.0)
            else:
                ahead = jnp.where(sub > j - s * SUBLANES, jnp.where(other >= slabs[s], 1.0, 0.0),
                                  jnp.where(other > slabs[s], 1.0, 0.0))
            ranks[s] = ranks[s] + ahead
    return jnp.concatenate(ranks, axis=0), side_results


def _attn_kernel(q_ref, slope_ref, kc_ref, vc_ref, ks_ref, vs_ref, kw_ref, vw_ref, gate_ref,
                 mcs_t_ref, expand_ref, chunk_of_ref, win_bias_ref, o_ref,
                 q4_ref, chosen_ref, s_ref, m_ref, acc_ref, gate_b_ref, slot_ref):
    qb = pl.program_id(2)
    q0 = qb * Q_BLOCK
    rows = HEADS_PER_GROUP * Q_BLOCK
    n_cmp = kc_ref.shape[2]
    n_sel = expand_ref.shape[1]

    qblk = q_ref[0]
    slope_cols = slope_ref[0]
    q4 = jnp.concatenate(
        [jnp.concatenate([qblk[:, r * HEAD_DIM:(r + 1) * HEAD_DIM],
                          jnp.broadcast_to(slope_cols[r:r + 1, :], (Q_BLOCK, HEAD_DIM))], axis=1)
         for r in range(HEADS_PER_GROUP)], axis=0)
    t_row = q0 + (lax.broadcasted_iota(jnp.int32, (rows, LANES), 0) & (Q_BLOCK - 1))
    tq = q0 + lax.broadcasted_iota(jnp.int32, (Q_BLOCK, 1), 0)

    cmp_end = lax.broadcasted_iota(jnp.int32, (1, n_cmp), 1) * CMP_STRIDE + (CMP_BLOCK - 1)
    pc4, ov = _pair_attend(_pair_scores(q4, kc_ref[0, 0]), vc_ref[0, 0],
                           jnp.where(tq >= cmp_end, 0.0, NEG_BIG))
    o_cmp = jnp.where(t_row >= CMP_BLOCK - 1, ov[:, :LANES] / ov[:, LANES:], 0.0)

    w0 = pl.multiple_of(jnp.maximum(q0 - WINDOW, 0), Q_BLOCK)
    half_rows = rows // 2

    def win_dot(i, j):
        keys = kw_ref[0, 0, pl.ds(w0 + j * SEL_CHUNK, SEL_CHUNK), :]
        return lax.dot_general(q4[i * half_rows:(i + 1) * half_rows], keys, _NT,
                               preferred_element_type=F32)

    win_dots = [functools.partial(win_dot, i, j)
                for i in range(2) for j in range(WIN_KEYS // SEL_CHUNK)]

    imp_l = lax.dot_general(mcs_t_ref[...], pc4, _NT, preferred_element_type=F32)
    imp = None
    for r in range(HEADS_PER_GROUP):
        cs = slice(r * Q_BLOCK, (r + 1) * Q_BLOCK)
        part = imp_l[:n_sel, cs] / imp_l[n_sel:n_sel + 1, cs]
        imp = part if imp is None else imp + part
    blk = lax.broadcasted_iota(jnp.int32, (n_sel, Q_BLOCK), 0)
    tq_l = q0 + lax.broadcasted_iota(jnp.int32, (n_sel, Q_BLOCK), 1)
    cur = tq_l >> 6
    forced = (blk == 0) | (blk == cur) | (blk == cur - 1)
    valid = blk * SEL_BLOCK <= tq_l
    score = jnp.where(forced, FORCED_SCORE, jnp.where(valid, imp, -1.0))
    rank, win_tiles = _block_ranks(score, win_dots)
    per_pair = WIN_KEYS // SEL_CHUNK
    win_scores = [jnp.concatenate(win_tiles[i * per_pair:(i + 1) * per_pair], axis=1)
                  for i in range(2)]
    chosen_t = jnp.where(valid, jnp.where(rank < float(min(SEL_TOPN, n_sel)), 1.0, 0.0), 0.0)
    chosen_ref[...] = chosen_t.T.astype(BF16)
    q4_ref[...] = q4

    n_chunks = expand_ref.shape[0]
    per_chunk = jnp.dot(chunk_of_ref[...], chosen_t.astype(BF16), preferred_element_type=F32)
    live = jnp.max(per_chunk, axis=1, keepdims=True) > 0.0
    weight = (1 << lax.broadcasted_iota(jnp.int32, (n_chunks, 1), 0)).astype(F32)
    live_bits = jnp.sum(jnp.where(live, weight, 0.0)).astype(jnp.int32)

    n_live = 0
    for c in range(n_chunks):
        slot_ref[n_live] = c
        n_live = n_live + ((live_bits >> c) & 1)

    vwin = vw_ref[0, 0, pl.ds(w0, WIN_KEYS), :]
    bias_w = win_bias_ref[jnp.minimum(qb, WINDOW // Q_BLOCK)]
    _, wv = _pair_attend(win_scores, vwin, bias_w)
    o_win = wv[:, :LANES] / wv[:, LANES:]

    for r in range(HEADS_PER_GROUP):
        for j in range(3):
            gate_b_ref[3 * r + j] = jnp.broadcast_to(gate_ref[0, :, 3 * r + j:3 * r + j + 1],
                                                     (Q_BLOCK, LANES))

    m_ref[...] = jnp.full(m_ref.shape, NEG_BIG, F32)

    def score_body(slot0, count):
        lane_max = [None] * HEADS_PER_GROUP
        starts, biases = [], []
        for u in range(count):
            c = slot_ref[slot0 + u]
            k0 = pl.multiple_of(c * SEL_CHUNK, SEL_CHUNK)
            hit = jnp.dot(chosen_ref[...], expand_ref[c], preferred_element_type=F32)
            pos = k0 + lax.broadcasted_iota(jnp.int32, (1, SEL_CHUNK), 1)
            biases.append(jnp.where(pos <= tq, (hit - 1.0) * (-NEG_BIG), NEG_BIG))
            starts.append(k0)
        for u in range(count):
            k = ks_ref[0, 0, pl.ds(starts[u], SEL_CHUNK), :]
            if count == 1:
                parts = _pair_scores(q4_ref[...], k)
            else:
                parts = [lax.dot_general(q4_ref[...], k, _NT, preferred_element_type=F32)]
            heads_per_part = HEADS_PER_GROUP // len(parts)
            for r in range(HEADS_PER_GROUP):
                lo = (r % heads_per_part) * Q_BLOCK
                s = parts[r // heads_per_part][lo:lo + Q_BLOCK] + biases[u]
                s_ref[slot0 + u, r * Q_BLOCK:(r + 1) * Q_BLOCK, :] = s
                mx = jnp.maximum(s[:, :LANES], s[:, LANES:])
                lane_max[r] = mx if lane_max[r] is None else jnp.maximum(lane_max[r], mx)
        for r in range(HEADS_PER_GROUP):
            rs = slice(r * Q_BLOCK, (r + 1) * Q_BLOCK)
            m_ref[rs, :] = jnp.maximum(m_ref[rs, :], lane_max[r])

    _for_each_group(n_live, score_body)
    m_ref[...] = jnp.broadcast_to(jnp.max(m_ref[...], axis=-1, keepdims=True), m_ref.shape)

    acc_ref[...] = jnp.zeros(acc_ref.shape, F32)

    def value_body(slot0, count):
        pv = None
        for u in range(count):
            slot = slot0 + u
            v0 = pl.multiple_of(slot_ref[slot] * SEL_CHUNK, SEL_CHUNK)
            v = vs_ref[0, 0, pl.ds(v0, SEL_CHUNK), :]
            p = jnp.concatenate([jnp.exp2(s_ref[slot, :, :LANES] - m_ref[...]),
                                 jnp.exp2(s_ref[slot, :, LANES:] - m_ref[...])],
                                axis=1).astype(BF16)
            if count == 1:
                half = rows // 2
                d = jnp.concatenate([jnp.dot(p[:half], v, preferred_element_type=F32),
                                     jnp.dot(p[half:], v, preferred_element_type=F32)], axis=0)
            else:
                d = jnp.dot(p, v, preferred_element_type=F32)
            pv = d if pv is None else pv + d
        acc_ref[...] += pv

    _for_each_group(n_live, value_body)
    o_sel = acc_ref[:, :LANES] / acc_ref[:, LANES:]

    lane = lax.broadcasted_iota(jnp.int32, (Q_BLOCK, LANES), 1)
    gated = []
    for r in range(HEADS_PER_GROUP):
        rs = slice(r * Q_BLOCK, (r + 1) * Q_BLOCK)
        gated.append(gate_b_ref[3 * r] * o_cmp[rs] + gate_b_ref[3 * r + 1] * o_sel[rs]
                     + gate_b_ref[3 * r + 2] * o_win[rs])
    o_ref[0] = jnp.concatenate(
        [jnp.where(lane < HEAD_DIM, gated[2 * i], gated[2 * i + 1])
         for i in range(HEADS_PER_GROUP // 2)], axis=-1)


def _attention(q, cmp_k, cmp_v, keys, values, gates):
    b, t, _ = q.shape
    n_cmp = cmp_k.shape[2]
    n_sel = t // SEL_BLOCK
    n_real_cmp = (t - CMP_BLOCK) // CMP_STRIDE + 1

    cs = np.arange(n_cmp)[None, :] * CMP_STRIDE
    js = np.arange(n_sel)[:, None] * SEL_BLOCK
    ov = np.minimum(cs + CMP_BLOCK, js + SEL_BLOCK) - np.maximum(cs, js)
    mcs_t = np.zeros((n_sel + SUBLANES, n_cmp), np.float32)
    mcs_t[:n_sel] = np.maximum(ov, 0).astype(np.float32) / CMP_BLOCK
    mcs_t[:n_sel, n_real_cmp:] = 0.0
    mcs_t[n_sel] = 1.0
    key_blk = (np.arange(t) // SEL_BLOCK).reshape(t // SEL_CHUNK, 1, SEL_CHUNK)
    expand = (key_blk == np.arange(n_sel)[None, :, None]).astype(np.float32)
    n_chunks = t // SEL_CHUNK
    chunk_of = (np.arange(n_sel)[None, :] // (SEL_CHUNK // SEL_BLOCK)
                == np.arange(n_chunks)[:, None]).astype(np.float32)
    rows = HEADS_PER_GROUP * Q_BLOCK
    dist = (np.arange(WINDOW // Q_BLOCK + 1)[:, None, None] * Q_BLOCK
            + np.arange(Q_BLOCK)[None, :, None] - np.arange(WIN_KEYS)[None, None, :])
    win_bias = np.where((dist >= 0) & (dist < WINDOW), 0.0, NEG_BIG).astype(np.float32)

    slope_l2 = LOG2E * np.power(2.0, -8.0 * np.arange(1, N_Q_HEADS + 1) / N_Q_HEADS)
    hi = slope_l2.astype(BF16).astype(np.float64)
    lo = (slope_l2 - hi).astype(BF16).astype(np.float64)
    cols = np.zeros((N_KV_GROUPS, SUBLANES, HEAD_DIM), np.float32)
    heads = cols[:, :HEADS_PER_GROUP].reshape(N_Q_HEADS, HEAD_DIM)
    heads[:, 0], heads[:, 1], heads[:, 2], heads[:, 3] = SEL_BLOCK * hi, SEL_BLOCK * lo, hi, lo
    cols[:, :HEADS_PER_GROUP] = heads.reshape(N_KV_GROUPS, HEADS_PER_GROUP, HEAD_DIM)

    def kv_spec(branch, rows, width):
        return pl.BlockSpec((1, 1, rows, width),
                            lambda bi, gi, qi, branch=branch: (bi, branch * N_KV_GROUPS + gi, 0, 0))

    return pl.pallas_call(
        _attn_kernel,
        grid=(b, N_KV_GROUPS, t // Q_BLOCK),
        in_specs=[
            pl.BlockSpec((1, Q_BLOCK, GROUP_WIDTH), lambda bi, gi, qi: (bi, qi, gi)),
            pl.BlockSpec((1, SUBLANES, HEAD_DIM), lambda bi, gi, qi: (gi, 0, 0)),
            kv_spec(0, n_cmp, AUG_DIM), kv_spec(0, n_cmp, WIDE_DIM),
            kv_spec(0, t, AUG_DIM), kv_spec(0, t, WIDE_DIM),
            kv_spec(1, t, AUG_DIM), kv_spec(1, t, WIDE_DIM),
            pl.BlockSpec((1, Q_BLOCK, LANES), lambda bi, gi, qi: (bi, qi, gi)),
            _const_spec(mcs_t.shape), _const_spec(expand.shape), _const_spec(chunk_of.shape),
            _const_spec(win_bias.shape),
        ],
        out_specs=pl.BlockSpec((1, Q_BLOCK, GROUP_WIDTH), lambda bi, gi, qi: (bi, qi, gi)),
        out_shape=jax.ShapeDtypeStruct((b, t, ATTN_WIDTH), F32),
        scratch_shapes=[
            pltpu.VMEM((rows, AUG_DIM), BF16),
            pltpu.VMEM((Q_BLOCK, n_sel), BF16),
            pltpu.VMEM((n_chunks, rows, SEL_CHUNK), F32),
            pltpu.VMEM((rows, LANES), F32),
            pltpu.VMEM((rows, WIDE_DIM), F32),
            pltpu.VMEM((3 * HEADS_PER_GROUP, Q_BLOCK, LANES), F32),
            pltpu.SMEM((n_chunks + 1,), jnp.int32),
        ],
        compiler_params=_params("parallel", "parallel", "arbitrary"),
        name="nsa_attn",
    )(q, jnp.asarray(cols, BF16), cmp_k, cmp_v, keys, values, keys, values, gates,
      jnp.asarray(mcs_t, BF16), jnp.asarray(expand, BF16), jnp.asarray(chunk_of, BF16),
      jnp.asarray(win_bias))


def _lru_kernel(xr_ref, xg_ref, cw_ref, cb_ref, wa_ref, ba_ref, wx_ref, bx_ref, lam_ref,
                g_ref, o_ref, xs_ref, a_ref, b_ref, h_ref):
    ti = pl.program_id(1)
    tt = xr_ref.shape[1]

    @pl.when(ti == 0)
    def _():
        xs_ref[...] = jnp.zeros_like(xs_ref)
        h_ref[...] = jnp.zeros_like(h_ref)

    n_groups = tt // SUBLANES
    as_groups = lambda v: v.reshape(n_groups, SUBLANES, LRU_WIDTH)
    sub = lax.broadcasted_iota(jnp.int32, (n_groups, SUBLANES, LRU_WIDTH), 1)

    x = as_groups(xr_ref[0])
    last = xs_ref[...]
    xc = cb_ref[...] + cw_ref[CONV_WIDTH - 1:CONV_WIDTH, :] * x
    for j in range(CONV_WIDTH - 1):
        lag = CONV_WIDTH - 1 - j
        rot = pltpu.roll(x, lag, axis=1)
        rot_before = jnp.concatenate([pltpu.roll(last, lag, axis=0)[None], rot[:-1]], axis=0)
        xc = xc + cw_ref[j:j + 1, :] * jnp.where(sub < lag, rot_before, rot)
    xs_ref[...] = x[n_groups - 1]
    xc = xc.reshape(tt, LRU_WIDTH)

    xb = xc.astype(BF16)
    r = jax.nn.sigmoid(jnp.dot(xb, wa_ref[...], preferred_element_type=F32) + ba_ref[...])
    i = jax.nn.sigmoid(jnp.dot(xb, wx_ref[...], preferred_element_type=F32) + bx_ref[...])
    neg_lam = -lam_ref[...]
    softplus = jnp.maximum(neg_lam, 0.0) + jnp.log1p(jnp.exp(-jnp.abs(neg_lam)))
    log_a = -LRU_C * r * softplus
    a = jnp.exp(log_a)
    th = jnp.tanh(log_a)
    mult = jnp.sqrt(jnp.maximum(-2.0 * th / (1.0 - th), 0.0))
    b = mult * i * xc

    a, b = as_groups(a), as_groups(b)
    for s in (1, 2, 4):
        ok = sub >= s
        a_prev = jnp.where(ok, pltpu.roll(a, s, axis=1), 1.0)
        b_prev = jnp.where(ok, pltpu.roll(b, s, axis=1), 0.0)
        b = a * b_prev + b
        a = a * a_prev
    a_ref[...] = a.reshape(tt, LRU_WIDTH)
    b_ref[...] = b.reshape(tt, LRU_WIDTH)

    def group_body(k, h):
        r0 = pl.multiple_of(k * SUBLANES, SUBLANES)
        h8 = a_ref[pl.ds(r0, SUBLANES), :] * h + b_ref[pl.ds(r0, SUBLANES), :]
        b_ref[pl.ds(r0, SUBLANES), :] = h8
        return jnp.broadcast_to(h8[SUBLANES - 1:SUBLANES, :], (SUBLANES, LRU_WIDTH))

    h_ref[...] = lax.fori_loop(0, tt // SUBLANES, group_body, h_ref[...])
    out = b_ref[...] * jax.nn.gelu(xg_ref[0])
    o_ref[0] = _rms(out, g_ref[...]).astype(BF16)


def _lru(xr, xg, conv_w, conv_b, wa, ba, wx, bx, lam, g):
    b, t, c = xr.shape
    tt = LRU_TOKENS
    tok = pl.BlockSpec((1, tt, c), lambda bi, ti: (bi, ti, 0))
    vec = _const_spec((1, c))
    return pl.pallas_call(
        _lru_kernel,
        grid=(b, t // tt),
        in_specs=[tok, tok, _const_spec((CONV_WIDTH, c)), vec, _const_spec((c, c)), vec,
                  _const_spec((c, c)), vec, vec, vec],
        out_specs=tok,
        out_shape=jax.ShapeDtypeStruct((b, t, c), BF16),
        scratch_shapes=[pltpu.VMEM((SUBLANES, c), F32), pltpu.VMEM((tt, c), F32),
                        pltpu.VMEM((tt, c), F32), pltpu.VMEM((SUBLANES, c), F32)],
        compiler_params=_params("parallel", "arbitrary"),
        name="rg_lru",
    )(xr, xg, conv_w, conv_b, wa, ba, wx, bx, lam, g)


def _mix_ffn_kernel(h_ref, attn_ref, lru_ref, ga_ref, mix_g_ref, wo_a_ref, wo_l_ref,
                    pre_g_ref, post_g_ref, wg_ref, wu_ref, wd_ref, o_ref, act_ref):
    ya = _rms(attn_ref[...], ga_ref[...]).astype(BF16)
    m = (jnp.dot(ya, wo_a_ref[...], preferred_element_type=F32)
         + jnp.dot(lru_ref[...], wo_l_ref[...], preferred_element_type=F32))
    h = h_ref[...] + _rms(m, mix_g_ref[...])
    o_ref[...] = _ffn_half_step(h, pre_g_ref, post_g_ref, wg_ref, wu_ref, wd_ref, act_ref)


def _mix_ffn(h, attn, lru, attn_g, mix_g, wo_a, wo_l, ffn_weights):
    n = h.shape[0]
    tm = FFN_TOKENS

    def tok(width):
        return pl.BlockSpec((tm, width), lambda i: (i, 0))

    return pl.pallas_call(
        _mix_ffn_kernel,
        grid=(n // tm,),
        in_specs=[tok(D_MODEL), tok(ATTN_WIDTH), tok(LRU_WIDTH), _const_spec((1, ATTN_WIDTH)),
                  _const_spec((1, D_MODEL)), _const_spec((ATTN_WIDTH, D_MODEL)),
                  _const_spec((LRU_WIDTH, D_MODEL))] + _ffn_weight_specs(),
        out_specs=tok(D_MODEL),
        out_shape=jax.ShapeDtypeStruct((n, D_MODEL), F32),
        scratch_shapes=[pltpu.VMEM((tm, D_FF), BF16)],
        compiler_params=_params("parallel"),
        name="mix_ffn",
    )(h, attn, lru, attn_g, mix_g, wo_a, wo_l, *ffn_weights)


def _pack_w_in(w_in):
    kv_cols = N_KV_GROUPS * HEAD_DIM
    gate_lo = ATTN_WIDTH + 6 * kv_cols
    gate_hi = gate_lo + 3 * N_Q_HEADS
    per_group = 3 * HEADS_PER_GROUP
    w_in = w_in.astype(BF16)
    pad = jnp.zeros((w_in.shape[0], LANES - per_group), w_in.dtype)
    gate_slabs = []
    for gi in range(N_KV_GROUPS):
        gate_slabs += [w_in[:, gate_lo + gi * per_group:gate_lo + (gi + 1) * per_group], pad]
    return jnp.concatenate([w_in[:, :gate_lo]] + gate_slabs + [w_in[:, gate_hi:]], axis=1)


def _block_diag(w):
    nb, d, e = w.shape
    eye = jnp.eye(nb, dtype=w.dtype)
    return jnp.einsum("nde,nm->ndme", w, eye).reshape(nb * d, nb * e).astype(BF16)


def _layer(h, p):
    b, t, d = h.shape
    n = b * t
    row = lambda v: v.reshape(1, -1)

    def ffn_weights(i):
        return (row(p[f"ffn{i}_pre_g"]), row(p[f"ffn{i}_post_g"]), p[f"ffn{i}_w_gate"].astype(BF16),
                p[f"ffn{i}_w_up"].astype(BF16), p[f"ffn{i}_w_down"].astype(BF16))

    h1, q, cmp_in, keys, values, gates, xr, xg = _ffn_proj(
        h, ffn_weights(1), row(p["mix_pre_g"]), _pack_w_in(p["w_in"]))
    half_block = lambda pe: pe.reshape(2, (CMP_BLOCK // 2) * HEAD_DIM)
    cmp_k, cmp_v = _compress(cmp_in, half_block(p["cmp_k_pe"]), p["cmp_k_w1"].astype(BF16),
                             p["cmp_k_w2"].astype(BF16), half_block(p["cmp_v_pe"]),
                             p["cmp_v_w1"].astype(BF16), p["cmp_v_w2"].astype(BF16))
    attn = _attention(q, cmp_k, cmp_v, keys, values, gates)
    lru = _lru(xr, xg, p["conv_w"], row(p["conv_b"]), _block_diag(p["lru_w_a"]),
               row(p["lru_b_a"]), _block_diag(p["lru_w_x"]), row(p["lru_b_x"]),
               row(p["lru_lambda"]), row(p["lru_out_g"]))
    w_out = p["w_out"].astype(BF16)
    h3 = _mix_ffn(h1.reshape(n, d), attn.reshape(n, ATTN_WIDTH), lru.reshape(n, LRU_WIDTH),
                  row(p["attn_out_g"]), row(p["mix_post_g"]),
                  w_out[:ATTN_WIDTH], w_out[ATTN_WIDTH:], ffn_weights(2))
    return h3.reshape(b, t, d)


_PARAM_NAMES = (
    "ffn1_pre_g", "ffn1_post_g", "ffn1_w_gate", "ffn1_w_up", "ffn1_w_down",
    "mix_pre_g", "mix_post_g", "w_in", "cmp_k_pe", "cmp_k_w1", "cmp_k_w2",
    "cmp_v_pe", "cmp_v_w1", "cmp_v_w2", "conv_w", "conv_b", "lru_w_a", "lru_b_a",
    "lru_w_x", "lru_b_x", "lru_lambda", "attn_out_g", "lru_out_g", "w_out",
    "ffn2_pre_g", "ffn2_post_g", "ffn2_w_gate", "ffn2_w_up", "ffn2_w_down",
)


def kernel(x, ffn1_pre_g, ffn1_post_g, ffn1_w_gate, ffn1_w_up, ffn1_w_down, mix_pre_g, mix_post_g, w_in, cmp_k_pe, cmp_k_w1, cmp_k_w2, cmp_v_pe, cmp_v_w1, cmp_v_w2, conv_w, conv_b, lru_w_a, lru_b_a, lru_w_x, lru_b_x, lru_lambda, attn_out_g, lru_out_g, w_out, ffn2_pre_g, ffn2_post_g, ffn2_w_gate, ffn2_w_up, ffn2_w_down):
    stacked = dict(zip(_PARAM_NAMES, (
        ffn1_pre_g, ffn1_post_g, ffn1_w_gate, ffn1_w_up, ffn1_w_down, mix_pre_g, mix_post_g,
        w_in, cmp_k_pe, cmp_k_w1, cmp_k_w2, cmp_v_pe, cmp_v_w1, cmp_v_w2, conv_w, conv_b,
        lru_w_a, lru_b_a, lru_w_x, lru_b_x, lru_lambda, attn_out_g, lru_out_g, w_out,
        ffn2_pre_g, ffn2_post_g, ffn2_w_gate, ffn2_w_up, ffn2_w_down)))
    h = x
    for layer in range(ffn1_pre_g.shape[0]):
        h = _layer(h, {k: v[layer] for k, v in stacked.items()})
    return h
```
